```python
import jax, jax.numpy as jnp
from jax import lax
import numpy as np

D_MODEL = 1024
BATCH = 4
SEQ = 8192
DEPTH = 1

CHUNK = 64
EPS = 1e-6
RET_HEADS = 4
RET_DK = 64
RET_DV = 128
RET_THETA = 10000.0
DSA_HEADS = 8
DSA_DH = 64
IDX_HEADS = 4
IDX_DH = 64
DSA_TOPK_MAX = 256
Q_BLOCK = 128
ROPE_THETA = 500000.0
ROPE_DIM = DSA_DH // 4
PEER_HEADS = 8
PEER_NKEYS = 128
PEER_N_EXPERTS = PEER_NKEYS * PEER_NKEYS
PEER_DKEY = 256
PEER_TOPK = 16
PEER_BLOCK = 128
RET_QK = RET_HEADS * RET_DK
RET_V = RET_HEADS * RET_DV
DSA_W = DSA_HEADS * DSA_DH
IDX_Q = IDX_HEADS * IDX_DH
N_BRANCH = 2
IN_SIZES = (RET_QK, RET_QK, RET_V, RET_V, DSA_W, DSA_W, DSA_W, IDX_Q, IDX_DH, IDX_HEADS, N_BRANCH * D_MODEL)
D_IN = 2 * RET_QK + 2 * RET_V + 3 * DSA_W + IDX_Q + IDX_DH + IDX_HEADS + N_BRANCH * D_MODEL

kernel_name = "hybrid_retention_dsa_peer_block"


def rms_norm(x, g):
    xf = x.astype(jnp.float32)
    y = xf * lax.rsqrt(jnp.mean(xf * xf, axis=-1, keepdims=True) + EPS)
    return (y * g.astype(jnp.float32)).astype(x.dtype)


def rope(x, pos, rot_dim, theta):
    half = rot_dim // 2
    inv = theta ** (-jnp.arange(half, dtype=jnp.float32) / half)
    ang = pos.astype(jnp.float32)[:, None] * inv[None, :]
    cos = jnp.cos(ang)[None, :, None, :]
    sin = jnp.sin(ang)[None, :, None, :]
    xr = x[..., :rot_dim].astype(jnp.float32)
    x1, x2 = xr[..., :half], xr[..., half:]
    rot = jnp.concatenate([x1 * cos - x2 * sin, x1 * sin + x2 * cos], axis=-1).astype(x.dtype)
    return jnp.concatenate([rot, x[..., rot_dim:]], axis=-1)


def split_cols(a, sizes):
    outs, off = [], 0
    for s in sizes:
        outs.append(a[..., off:off + s])
        off += s
    return outs


def retention(q, k, v, gate, gn):
    f32 = jnp.float32
    B, L, H, dk = q.shape
    dv = v.shape[-1]
    C = CHUNK
    NC = L // C
    log_g = jnp.log(1.0 - 2.0 ** (-5.0 - jnp.arange(H, dtype=f32)))
    i = jnp.arange(C, dtype=f32)
    qc = q.astype(f32).reshape(B, NC, C, H, dk) * (dk ** -0.5)
    kc = k.astype(f32).reshape(B, NC, C, H, dk)
    vc = v.astype(f32).reshape(B, NC, C, H, dv)
    intra_decay = jnp.exp(log_g[:, None, None] * jnp.abs(i[:, None] - i[None, :]))
    scores = jnp.einsum('bnihd,bnjhd->bnhij', qc, kc) * intra_decay
    intra = jnp.einsum('bnhij,bnjhe->bnihe', scores, vc)
    k_decay = jnp.exp(log_g[None, :] * (C - 1 - i)[:, None])
    kv = jnp.einsum('bnjhd,bnjhe->nbhde', kc * k_decay[:, :, None], vc)
    chunk_decay = jnp.exp(log_g * C)[None, :, None, None]

    def step(state, kv_n):
        return chunk_decay * state + kv_n, state

    _, prev = lax.scan(step, jnp.zeros_like(kv[0]), kv)
    q_decay = jnp.exp(log_g[None, :] * (i + 1.0)[:, None])
    cross = jnp.einsum('bnihd,nbhde->bnihe', qc * q_decay[:, :, None], prev)
    y = (intra + cross).reshape(B, L, H, dv)
    mu = jnp.mean(y, axis=-1, keepdims=True)
    var = jnp.mean(jnp.square(y - mu), axis=-1, keepdims=True)
    y = ((y - mu) * lax.rsqrt(var + EPS)).reshape(B, L, H * dv) * gn.astype(f32)
    return (jax.nn.swish(gate.astype(f32)) * y).astype(gate.dtype)


def dsa_attention(q, k, v, q_idx, k_idx, w_idx, topk):
    f32 = jnp.float32
    B, L, H, dh = q.shape
    NB = L // Q_BLOCK
    key_chunk = jnp.arange(L) // CHUNK

    def to_blocks(a):
        return jnp.moveaxis(a.reshape(B, NB, Q_BLOCK, *a.shape[2:]), 1, 0)

    def block(args):
        qb, qib, wb, qpos = args
        qchunk = qpos // CHUNK
        admissible = key_chunk[None, :] <= qchunk[:, None]
        rel = jax.nn.relu(jnp.einsum('bqhd,bsd->bqhs', qib, k_idx).astype(f32))
        iscore = jnp.einsum('bqhs,bqh->bqs', rel, wb.astype(f32))
        iscore = jnp.where(admissible[None], iscore, -jnp.inf)
        _, sel = lax.top_k(iscore, topk)
        valid = (sel // CHUNK) <= qchunk[None, :, None]
        k_sel = jax.vmap(lambda kb, ib: kb[ib])(k, sel)
        v_sel = jax.vmap(lambda vb, ib: vb[ib])(v, sel)
        logits = jnp.einsum('bqhd,bqkhd->bhqk', qb, k_sel).astype(f32) * (dh ** -0.5)
        logits = jnp.where(valid[:, None], logits, -jnp.inf)
        p = jax.nn.softmax(logits, axis=-1).astype(v.dtype)
        return jnp.einsum('bhqk,bqkhd->bqhd', p, v_sel)

    qpos = jnp.arange(L).reshape(NB, Q_BLOCK)
    out = lax.map(block, (to_blocks(q), to_blocks(q_idx), to_blocks(w_idx), qpos))
    return jnp.moveaxis(out, 0, 1).reshape(B, L, H * dh)


def peer_ffn(h, w_q, sub_keys, u_tab, v_tab):
    f32 = jnp.float32
    B, L, D = h.shape
    T = PEER_BLOCK
    K = PEER_TOPK
    tokens = h.reshape(-1, T, D)

    def block(xb):
        q = (xb @ w_q).reshape(T, PEER_HEADS, 2, PEER_DKEY // 2)
        s = jnp.einsum('thcd,hckd->thck', q, sub_keys).astype(f32)
        s_top, i_top = lax.top_k(s, K)
        cand = s_top[:, :, 0, :, None] + s_top[:, :, 1, None, :]
        cand_idx = i_top[:, :, 0, :, None] * PEER_NKEYS + i_top[:, :, 1, None, :]
        best, pos = lax.top_k(cand.reshape(T, PEER_HEADS, K * K), K)
        expert = jnp.take_along_axis(cand_idx.reshape(T, PEER_HEADS, K * K), pos, axis=-1)
        g = jax.nn.softmax(best, axis=-1)
        u_sel = u_tab[expert]
        a = jax.nn.gelu(jnp.einsum('thkd,td->thk', u_sel, xb).astype(f32), approximate=False)
        return jnp.einsum('thk,thkd->td', (g * a).astype(xb.dtype), v_tab[expert])

    return lax.map(block, tokens).reshape(B, L, D)


def setup_inputs(seed: int = 0) -> dict:
    key = jax.random.key(seed)
    ks = jax.random.split(key, 16)
    D = D_MODEL
    nrm = lambda k, shape, scale: jax.random.normal(k, shape, jnp.float32) * scale
    return {
        "x": nrm(ks[0], (BATCH, SEQ, D), 1.0),
        "attn_norm": 1.0 + nrm(ks[1], (DEPTH, D), 0.01),
        "w_in": nrm(ks[2], (DEPTH, D, D_IN), D ** -0.5),
        "ret_gn": 1.0 + nrm(ks[3], (DEPTH, RET_V), 0.01),
        "w_ret_o": nrm(ks[4], (DEPTH, RET_V, D), RET_V ** -0.5),
        "w_dsa_o": nrm(ks[5], (DEPTH, DSA_W, D), DSA_W ** -0.5),
        "w_out": nrm(ks[6], (DEPTH, D, D), D ** -0.5),
        "ffn_norm": 1.0 + nrm(ks[7], (DEPTH, D), 0.01),
        "peer_wq": nrm(ks[8], (DEPTH, D, PEER_HEADS * PEER_DKEY), D ** -0.5),
        "peer_subkeys": nrm(ks[9], (DEPTH, PEER_HEADS, 2, PEER_NKEYS, PEER_DKEY // 2), (PEER_DKEY // 2) ** -0.5),
        "peer_u": nrm(ks[10], (DEPTH, PEER_N_EXPERTS, D), D ** -0.5),
        "peer_v": nrm(ks[11], (DEPTH, PEER_N_EXPERTS, D), PEER_HEADS ** -0.5),
        "final_norm": 1.0 + nrm(ks[12], (D,), 0.01),
    }


def reference(x, attn_norm, w_in, ret_gn, w_ret_o, w_dsa_o, w_out, ffn_norm, peer_wq, peer_subkeys, peer_u, peer_v, final_norm):
    B, L, D = x.shape
    pos = jnp.arange(L)
    topk = min(DSA_TOPK_MAX, L // 4)
    for layer in range(DEPTH):
        h = rms_norm(x, attn_norm[layer])
        proj = h @ w_in[layer]
        (rq, rk, rv, rg, dq, dk, dv, iq, ik, iw, gates) = split_cols(proj, IN_SIZES)
        rq = rope(rq.reshape(B, L, RET_HEADS, RET_DK), pos, RET_DK, RET_THETA)
        rk = rope(rk.reshape(B, L, RET_HEADS, RET_DK), pos, RET_DK, RET_THETA)
        y_ret = retention(rq, rk, rv.reshape(B, L, RET_HEADS, RET_DV), rg, ret_gn[layer])
        dq = rope(dq.reshape(B, L, DSA_HEADS, DSA_DH), pos, ROPE_DIM, ROPE_THETA)
        dk = rope(dk.reshape(B, L, DSA_HEADS, DSA_DH), pos, ROPE_DIM, ROPE_THETA)
        iq = rope(iq.reshape(B, L, IDX_HEADS, IDX_DH), pos, IDX_DH // 4, ROPE_THETA)
        ik = rope(ik.reshape(B, L, 1, IDX_DH), pos, IDX_DH // 4, ROPE_THETA).reshape(B, L, IDX_DH)
        iw = iw * ((IDX_HEADS ** -0.5) * (IDX_DH ** -0.5))
        y_dsa = dsa_attention(dq, dk, dv.reshape(B, L, DSA_HEADS, DSA_DH), iq, ik, iw, topk)
        g_ret, g_dsa = split_cols(jax.nn.sigmoid(gates), (D, D))
        merged = g_ret * (y_ret @ w_ret_o[layer]) + g_dsa * (y_dsa @ w_dsa_o[layer])
        x = x + merged @ w_out[layer]
        h2 = rms_norm(x, ffn_norm[layer])
        x = x + peer_ffn(h2, peer_wq[layer], peer_subkeys[layer], peer_u[layer], peer_v[layer])
    return rms_norm(x, final_norm)
```

```python
import functools

import jax
import jax.numpy as jnp
import numpy as np
from jax import lax
from jax.experimental import pallas as pl
from jax.experimental.pallas import tpu as pltpu

F32 = jnp.float32
BF16 = jnp.bfloat16
I32 = jnp.int32

CHUNK = 64
EPS = 1e-6
RET_HEADS, RET_DK, RET_DV, RET_THETA = 4, 64, 128, 10000.0
DSA_HEADS, DSA_DH, IDX_HEADS, IDX_DH = 8, 64, 4, 64
DSA_TOPK_MAX = 256
ROPE_THETA, ROPE_DIM = 500000.0, 16
PEER_HEADS, PEER_NKEYS, PEER_DKEY, PEER_TOPK = 8, 128, 256, 16

LANES = 128
VMEM_LIMIT = 56 * 1024 * 1024

RET_QK = RET_HEADS * RET_DK
RET_V = RET_HEADS * RET_DV
DSA_W = DSA_HEADS * DSA_DH
IDX_Q = IDX_HEADS * IDX_DH

TM_IN = 512
TB_RET = 256
QB = 128
KB = 512
TM_MERGE = 512
TM_ROUTE = 256
TM_PEER = 512
IG_PEER = 8

NEG_BIG = -1e30
M_INIT = -1e20
INT_MIN = -(2 ** 31)

_OFF = {}
_o = 0
for _name, _w in (("rq", RET_QK), ("rk", RET_QK), ("rv", RET_V), ("rg", RET_V), ("dq", DSA_W), ("dk", DSA_W),
                  ("dv", DSA_W), ("iq", IDX_Q), ("ik", LANES), ("iw", LANES), ("gates", None)):
    _OFF[_name] = _o
    if _w is not None:
        _o += _w


def _params(sem):
    return pltpu.CompilerParams(dimension_semantics=sem, vmem_limit_bytes=VMEM_LIMIT)


def _rope_chunk(p, cos, sin, half):
    lane = lax.broadcasted_iota(I32, p.shape, 1) % 64
    partner = jnp.where(lane < half, pltpu.roll(p, LANES - half, 1), pltpu.roll(p, half, 1))
    return p * cos + partner * sin


def _inproj_kernel(x_ref, g_ref, w_ref, cr_ref, sr_ref, cp_ref, sp_ref,
                   rq_ref, rk_ref, rv_ref, rg_ref, dq_ref, dkT_ref, dv_ref, iq_ref, ikT_ref, iw_ref, gate_ref,
                   *, d_model):
    x = x_ref[...]
    ms = jnp.mean(x * x, axis=-1, keepdims=True)
    h = (x * lax.rsqrt(ms + EPS) * g_ref[...]).astype(BF16)

    def proj(off, width):
        return jnp.dot(h, w_ref[:, off:off + width], preferred_element_type=F32)

    cr, sr, cp, sp = cr_ref[...], sr_ref[...], cp_ref[...], sp_ref[...]
    qk_scale = RET_DK ** -0.5
    for c in range(RET_QK // LANES):
        sl = slice(c * LANES, (c + 1) * LANES)
        rq_ref[:, sl] = _rope_chunk(proj(_OFF["rq"] + c * LANES, LANES), cr, sr, RET_DK // 2) * qk_scale
        rk_ref[:, sl] = _rope_chunk(proj(_OFF["rk"] + c * LANES, LANES), cr, sr, RET_DK // 2)
    rv_ref[...] = proj(_OFF["rv"], RET_V).astype(BF16)
    rg_ref[...] = proj(_OFF["rg"], RET_V)
    dh_scale = DSA_DH ** -0.5
    for c in range(DSA_W // LANES):
        sl = slice(c * LANES, (c + 1) * LANES)
        dq_ref[:, sl] = (_rope_chunk(proj(_OFF["dq"] + c * LANES, LANES), cp, sp, ROPE_DIM // 2) * dh_scale).astype(BF16)
        dk = _rope_chunk(proj(_OFF["dk"] + c * LANES, LANES), cp, sp, ROPE_DIM // 2)
        dkT_ref[sl, :] = dk.T.astype(BF16)
    dv_ref[...] = proj(_OFF["dv"], DSA_W).astype(BF16)
    for c in range(IDX_Q // LANES):
        sl = slice(c * LANES, (c + 1) * LANES)
        iq_ref[:, sl] = _rope_chunk(proj(_OFF["iq"] + c * LANES, LANES), cp, sp, ROPE_DIM // 2).astype(BF16)
    ik = _rope_chunk(proj(_OFF["ik"], LANES), cp, sp, ROPE_DIM // 2)
    ikT_ref[...] = ik.T[:IDX_DH, :].astype(BF16)
    iw_ref[...] = proj(_OFF["iw"], LANES) * ((IDX_HEADS ** -0.5) * (IDX_DH ** -0.5))
    gate_ref[...] = proj(_OFF["gates"], 2 * d_model)


def _rope_tables(L, half, theta, pass_dims):
    inv = theta ** (-jnp.arange(half, dtype=F32) / half)
    ang = jnp.arange(L).astype(F32)[:, None] * inv[None, :]
    cos, sin = jnp.cos(ang), jnp.sin(ang)
    ones = jnp.ones((L, pass_dims), F32)
    zeros = jnp.zeros((L, pass_dims), F32)
    cos64 = jnp.concatenate([cos, cos, ones], axis=-1)
    sin64 = jnp.concatenate([-sin, sin, zeros], axis=-1)
    return jnp.tile(cos64, (1, 2)), jnp.tile(sin64, (1, 2))


def _inproj(x, attn_norm, w_in):
    B, L, D = x.shape
    tm = TM_IN
    nl = L // tm
    sizes = (RET_QK, RET_QK, RET_V, RET_V, DSA_W, DSA_W, DSA_W, IDX_Q, IDX_DH, IDX_HEADS, 2 * D)
    offs = np.concatenate([[0], np.cumsum(sizes)])
    cols = [w_in[:, offs[i]:offs[i + 1]] for i in range(len(sizes))]
    cols[8] = jnp.pad(cols[8], ((0, 0), (0, LANES - IDX_DH)))
    cols[9] = jnp.pad(cols[9], ((0, 0), (0, LANES - IDX_HEADS)))
    w = jnp.concatenate(cols, axis=1).astype(BF16)
    wtot = w.shape[1]
    cr, sr = _rope_tables(L, RET_DK // 2, RET_THETA, 0)
    cp, sp = _rope_tables(L, ROPE_DIM // 2, ROPE_THETA, DSA_DH - ROPE_DIM)

    tok = lambda width: pl.BlockSpec((None, tm, width), lambda b, l: (b, l, 0))
    tab = pl.BlockSpec((tm, LANES), lambda b, l: (l, 0))
    out_shape = (
        jax.ShapeDtypeStruct((B, L, RET_QK), F32),
        jax.ShapeDtypeStruct((B, L, RET_QK), F32),
        jax.ShapeDtypeStruct((B, L, RET_V), BF16),
        jax.ShapeDtypeStruct((B, L, RET_V), F32),
        jax.ShapeDtypeStruct((B, L, DSA_W), BF16),
        jax.ShapeDtypeStruct((B, nl, DSA_W, tm), BF16),
        jax.ShapeDtypeStruct((B, L, DSA_W), BF16),
        jax.ShapeDtypeStruct((B, L, IDX_Q), BF16),
        jax.ShapeDtypeStruct((B, nl, IDX_DH, tm), BF16),
        jax.ShapeDtypeStruct((B, L, LANES), F32),
        jax.ShapeDtypeStruct((B, L, 2 * D), F32),
    )
    out_specs = (
        tok(RET_QK), tok(RET_QK), tok(RET_V), tok(RET_V), tok(DSA_W),
        pl.BlockSpec((None, None, DSA_W, tm), lambda b, l: (b, l, 0, 0)),
        tok(DSA_W), tok(IDX_Q),
        pl.BlockSpec((None, None, IDX_DH, tm), lambda b, l: (b, l, 0, 0)),
        tok(LANES), tok(2 * D),
    )
    return pl.pallas_call(
        functools.partial(_inproj_kernel, d_model=D),
        grid=(B, nl),
        in_specs=[tok(D), pl.BlockSpec((1, D), lambda b, l: (0, 0)),
                  pl.BlockSpec((D, wtot), lambda b, l: (0, 0)), tab, tab, tab, tab],
        out_specs=out_specs,
        out_shape=out_shape,
        compiler_params=_params(("parallel", "parallel")),
        name="inproj",
    )(x, attn_norm.reshape(1, D), w, cr, sr, cp, sp)


def _retention_kernel(q_ref, k_ref, v_ref, g_ref, gn_ref, dm_ref, qd_ref, kd_ref, cd_ref, y_ref, state_ref):
    @pl.when(pl.program_id(1) == 0)
    def _():
        state_ref[...] = jnp.zeros_like(state_ref)

    for h in range(RET_HEADS):
        q = q_ref[:, h * RET_DK:(h + 1) * RET_DK]
        k = k_ref[:, h * RET_DK:(h + 1) * RET_DK]
        v = v_ref[:, h * RET_DV:(h + 1) * RET_DV]
        s = lax.dot_general(q.astype(BF16), k.astype(BF16), (((1,), (1,)), ((), ())), preferred_element_type=F32)
        intra = jnp.dot((s * dm_ref[h]).astype(BF16), v, preferred_element_type=F32)
        state = state_ref[h]
        cross = jnp.dot((q * qd_ref[h]).astype(BF16), state.astype(BF16), preferred_element_type=F32)
        kv = lax.dot_general((k * kd_ref[h]).astype(BF16), v, (((0,), (0,)), ((), ())), preferred_element_type=F32)
        state_ref[h] = cd_ref[h] * state + kv
        y = intra + cross
        mu = jnp.mean(y, axis=-1, keepdims=True)
        yc = y - mu
        var = jnp.mean(yc * yc, axis=-1, keepdims=True)
        sl = slice(h * RET_DV, (h + 1) * RET_DV)
        yn = yc * lax.rsqrt(var + EPS) * gn_ref[:, sl]
        g = g_ref[:, sl]
        y_ref[:, sl] = (g * jax.nn.sigmoid(g) * yn).astype(y_ref.dtype)


def _retention(rq, rk, rv, rg, ret_gn):
    B, L, _ = rq.shape
    tb = TB_RET
    log_g = jnp.log(1.0 - 2.0 ** (-5.0 - jnp.arange(RET_HEADS, dtype=F32)))
    i = jnp.arange(tb, dtype=F32)
    ci = jnp.arange(tb) // CHUNK
    visible = (ci[None, :] <= ci[:, None]).astype(F32)
    dm = jnp.exp(log_g[:, None, None] * jnp.abs(i[:, None] - i[None, :])) * visible[None]
    ones = jnp.ones((1, 1, RET_DK), F32)
    qd = jnp.exp(log_g[:, None] * (i + 1.0)[None, :])[:, :, None] * ones
    kd = jnp.exp(log_g[:, None] * (tb - 1.0 - i)[None, :])[:, :, None] * ones
    cd = jnp.exp(log_g * tb)[:, None, None] * jnp.ones((1, RET_DK, RET_DV), F32)

    tok = lambda width: pl.BlockSpec((None, tb, width), lambda b, l: (b, l, 0))
    full = lambda shape: pl.BlockSpec(shape, lambda b, l: (0,) * len(shape))
    return pl.pallas_call(
        _retention_kernel,
        grid=(B, L // tb),
        in_specs=[tok(RET_QK), tok(RET_QK), tok(RET_V), tok(RET_V), full((1, RET_V)),
                  full((RET_HEADS, tb, tb)), full((RET_HEADS, tb, RET_DK)), full((RET_HEADS, tb, RET_DK)),
                  full((RET_HEADS, RET_DK, RET_DV))],
        out_specs=tok(RET_V),
        out_shape=jax.ShapeDtypeStruct((B, L, RET_V), BF16),
        scratch_shapes=[pltpu.VMEM((RET_HEADS, RET_DK, RET_DV), F32)],
        compiler_params=_params(("parallel", "arbitrary")),
        name="retention",
    )(rq, rk, rv, rg, ret_gn.reshape(1, RET_V), dm, qd, kd, cd)


def _dsa_kernel(iq_ref, iw_ref, ikT_ref, dq_ref, dkT_ref, dv_ref, tri_ref, y_ref,
                keys_ref, qm_ref, m_ref, l_ref, acc_ref, *, topk):
    qb = pl.program_id(1)
    nkb = (qb * QB) // KB + 1
    n_sub = KB // LANES

    row = lax.broadcasted_iota(I32, (QB, KB), 0)
    col = lax.broadcasted_iota(I32, (QB, KB), 1)
    q_chunk = (qb * QB + row) // CHUNK

    iq = iq_ref[...]
    iw = iw_ref[...]

    def score_block(kb, carry):
        ik = ikT_ref[kb]
        acc = jnp.zeros((QB, KB), F32)
        for h in range(IDX_HEADS):
            r = jnp.dot(iq[:, h * IDX_DH:(h + 1) * IDX_DH], ik, preferred_element_type=F32)
            acc = acc + iw[:, h:h + 1] * jnp.maximum(r, 0.0)
        acc = acc + 0.0
        bits = pltpu.bitcast(acc, I32)
        key = jnp.where(bits < 0, bits ^ 0x7FFFFFFF, bits)
        admissible = (kb * KB + col) // CHUNK <= q_chunk
        keys_ref[kb] = jnp.where(admissible, key, INT_MIN)
        return carry

    lax.fori_loop(0, nkb, score_block, 0)

    def count_ge(cand):
        cand_b = jnp.broadcast_to(cand, (QB, LANES))

        def body(kb, cnt):
            for c in range(n_sub):
                kc = keys_ref[kb, :, c * LANES:(c + 1) * LANES]
                cnt = cnt + jnp.where(kc >= cand_b, 1, 0)
            return cnt

        cnt = lax.fori_loop(0, nkb, body, jnp.zeros((QB, LANES), I32))
        return jnp.sum(cnt, axis=1, keepdims=True)

    zero = jnp.zeros((QB, 1), I32)
    t0 = jnp.where(count_ge(zero) >= topk, zero, zero + INT_MIN)

    def bit_step(n, t):
        cand = t + jnp.left_shift(jnp.int32(1), 30 - n)
        return jnp.where(count_ge(cand) >= topk, cand, t)

    thr = lax.fori_loop(0, 31, bit_step, t0)
    n_gt = count_ge(thr + 1)
    need = jnp.where(thr == INT_MIN, 0, topk - n_gt).astype(F32)

    lane = lax.broadcasted_iota(I32, (QB, LANES), 1)
    for h in range(DSA_HEADS):
        qp = dq_ref[:, (h // 2) * LANES:(h // 2 + 1) * LANES]
        own = (lane < DSA_DH) if h % 2 == 0 else (lane >= DSA_DH)
        qm_ref[h] = jnp.where(own, qp, jnp.zeros_like(qp))
    m_ref[...] = jnp.full(m_ref.shape, M_INIT, F32)
    l_ref[...] = jnp.zeros(l_ref.shape, F32)
    acc_ref[...] = jnp.zeros(acc_ref.shape, F32)

    def attend_block(kb, n_eq_before):
        key = keys_ref[kb]
        eq = key == thr
        eq_f = jnp.where(eq, 1.0, 0.0)
        prefix = jnp.dot(eq_f.astype(BF16), tri_ref[...], preferred_element_type=F32)
        keep_tie = (n_eq_before + prefix) < need
        bias = jnp.where(key > thr, 0.0, jnp.where(eq, jnp.where(keep_tie, 0.0, NEG_BIG), NEG_BIG))
        v_blk = dv_ref[pl.ds(pl.multiple_of(kb * KB, KB), KB), :]
        for h in range(DSA_HEADS):
            pair = h // 2
            kT = dkT_ref[kb, pair * LANES:(pair + 1) * LANES, :]
            s = jnp.dot(qm_ref[h], kT, preferred_element_type=F32) + bias
            m_prev = m_ref[h]
            m_new = jnp.maximum(m_prev, jnp.max(s, axis=1, keepdims=True))
            p = jnp.exp(s - m_new)
            alpha = jnp.exp(m_prev - m_new)
            l_ref[h] = alpha * l_ref[h] + jnp.sum(p, axis=1, keepdims=True)
            m_ref[h] = m_new
            pv = jnp.dot(p.astype(BF16), v_blk[:, pair * LANES:(pair + 1) * LANES], preferred_element_type=F32)
            acc_ref[h] = alpha * acc_ref[h] + pv
        return n_eq_before + jnp.sum(eq_f, axis=1, keepdims=True)

    lax.fori_loop(0, nkb, attend_block, jnp.zeros((QB, 1), F32))

    for pair in range(DSA_HEADS // 2):
        even = acc_ref[2 * pair] / l_ref[2 * pair]
        odd = acc_ref[2 * pair + 1] / l_ref[2 * pair + 1]
        y_ref[:, pair * LANES:(pair + 1) * LANES] = jnp.where(lane < DSA_DH, even, odd).astype(y_ref.dtype)


def _dsa(iq, iw, ikT, dq, dkT, dv):
    B, L, _ = dq.shape
    nkb_all = L // KB
    topk = min(DSA_TOPK_MAX, L // 4)
    r = jnp.arange(KB)
    tri = (r[:, None] < r[None, :]).astype(BF16)
    qtok = lambda width: pl.BlockSpec((None, QB, width), lambda b, q: (b, q, 0))
    return pl.pallas_call(
        functools.partial(_dsa_kernel, topk=topk),
        grid=(B, L // QB),
        in_specs=[qtok(IDX_Q), qtok(LANES),
                  pl.BlockSpec((None, nkb_all, IDX_DH, KB), lambda b, q: (b, 0, 0, 0)),
                  qtok(DSA_W),
                  pl.BlockSpec((None, nkb_all, DSA_W, KB), lambda b, q: (b, 0, 0, 0)),
                  pl.BlockSpec((None, L, DSA_W), lambda b, q: (b, 0, 0)),
                  pl.BlockSpec((KB, KB), lambda b, q: (0, 0))],
        out_specs=qtok(DSA_W),
        out_shape=jax.ShapeDtypeStruct((B, L, DSA_W), BF16),
        scratch_shapes=[pltpu.VMEM((nkb_all, QB, KB), I32),
                        pltpu.VMEM((DSA_HEADS, QB, LANES), BF16),
                        pltpu.VMEM((DSA_HEADS, QB, 1), F32),
                        pltpu.VMEM((DSA_HEADS, QB, 1), F32),
                        pltpu.VMEM((DSA_HEADS, QB, LANES), F32)],
        compiler_params=_params(("parallel", "arbitrary")),
        name="dsa",
    )(iq, iw, ikT, dq, dkT, dv, tri)


def _merge_kernel(yr_ref, yd_ref, gate_ref, x_ref, wr_ref, wd_ref, wo_ref, fn_ref, x1_ref, h2T_ref, *, d_model):
    g = gate_ref[...]
    br = jnp.dot(yr_ref[...], wr_ref[...], preferred_element_type=F32)
    bd = jnp.dot(yd_ref[...], wd_ref[...], preferred_element_type=F32)
    merged = jax.nn.sigmoid(g[:, :d_model]) * br + jax.nn.sigmoid(g[:, d_model:]) * bd
    x1 = x_ref[...] + jnp.dot(merged.astype(BF16), wo_ref[...], preferred_element_type=F32)
    x1_ref[...] = x1
    ms = jnp.mean(x1 * x1, axis=-1, keepdims=True)
    h2 = x1 * lax.rsqrt(ms + EPS) * fn_ref[...]
    h2T_ref[...] = h2.T.astype(BF16)


def _merge(y_ret, y_dsa, gates, x, w_ret_o, w_dsa_o, w_out, ffn_norm):
    N, D = x.shape
    tm = TM_MERGE
    tok = lambda width: pl.BlockSpec((tm, width), lambda t: (t, 0))
    full = lambda shape: pl.BlockSpec(shape, lambda t: (0, 0))
    return pl.pallas_call(
        functools.partial(_merge_kernel, d_model=D),
        grid=(N // tm,),
        in_specs=[tok(RET_V), tok(DSA_W), tok(2 * D), tok(D),
                  full((RET_V, D)), full((DSA_W, D)), full((D, D)), full((1, D))],
        out_specs=(tok(D), pl.BlockSpec((D, tm), lambda t: (0, t))),
        out_shape=(jax.ShapeDtypeStruct((N, D), F32), jax.ShapeDtypeStruct((D, N), BF16)),
        compiler_params=_params(("parallel",)),
        name="merge",
    )(y_ret, y_dsa, gates, x, w_ret_o.astype(BF16), w_dsa_o.astype(BF16), w_out.astype(BF16),
      ffn_norm.reshape(1, D))


_CAND = [(a, b) for a in range(PEER_TOPK) for b in range(PEER_TOPK) if (a + 1) * (b + 1) <= PEER_TOPK]


def _top_values(s, n):
    rows = lax.broadcasted_iota(I32, s.shape, 0)
    out = []
    for _ in range(n):
        m = jnp.max(s, axis=0, keepdims=True)
        first = jnp.min(jnp.where(s == m, rows, s.shape[0]), axis=0, keepdims=True)
        s = jnp.where(rows == first, -jnp.inf, s)
        out.append(m)
    return out


def _peer_route_kernel(h2T_ref, wqT_ref, sk_ref, s1_ref, p1_ref, th_ref, p0_ref):
    h2T = h2T_ref[...]
    for h in range(PEER_HEADS):
        sT = []
        for c in range(2):
            r0 = (h * 2 + c) * PEER_NKEYS
            qT = jnp.dot(wqT_ref[r0:r0 + PEER_NKEYS, :], h2T, preferred_element_type=F32)
            sT.append(jnp.dot(sk_ref[h, c], qT.astype(BF16), preferred_element_type=F32))
        v0 = _top_values(sT[0], PEER_TOPK)
        v1 = _top_values(sT[1], PEER_TOPK)
        cand = jnp.concatenate([v0[a] + v1[b] for a, b in _CAND], axis=0)
        best = _top_values(cand, PEER_TOPK)
        tau = best[PEER_TOPK - 1]
        z = best[0] - best[0]
        for k in range(PEER_TOPK):
            z = z + jnp.exp(best[k] - best[0])
        theta = jnp.full(sT[0].shape, jnp.inf, F32)
        for b in range(PEER_TOPK):
            theta = jnp.minimum(theta, jnp.where(sT[0] + v1[b] >= tau, v1[b], jnp.inf))
        s1_ref[h] = sT[1]
        p1_ref[h] = jnp.exp(sT[1] - v1[0])
        th_ref[h] = theta
        p0_ref[h] = jnp.exp(sT[0] - v0[0]) / z


def _peer_route(h2T, peer_wq, peer_subkeys):
    D, N = h2T.shape
    tm = TM_ROUTE
    wqT = peer_wq.T.astype(BF16)
    sk = peer_subkeys.astype(BF16)
    spec = pl.BlockSpec((PEER_HEADS, PEER_NKEYS, tm), lambda t: (0, 0, t))
    shape = jax.ShapeDtypeStruct((PEER_HEADS, PEER_NKEYS, N), F32)
    return pl.pallas_call(
        _peer_route_kernel,
        grid=(N // tm,),
        in_specs=[pl.BlockSpec((D, tm), lambda t: (0, t)),
                  pl.BlockSpec(wqT.shape, lambda t: (0, 0)),
                  pl.BlockSpec(sk.shape, lambda t: (0, 0, 0, 0))],
        out_specs=(spec, spec, spec, spec),
        out_shape=(shape, shape, shape, shape),
        compiler_params=_params(("parallel",)),
        name="peer_route",
    )(h2T, wqT, sk)


def _peer_dense_kernel(h2T_ref, s1_ref, p1_ref, th_ref, p0_ref, u_ref, vT_ref, x1_ref, fn_ref, out_ref,
                       acc_ref, a_ref, g_ref):
    ig = pl.program_id(1)
    tm = h2T_ref.shape[1]

    @pl.when(ig == 0)
    def _():
        acc_ref[...] = jnp.zeros_like(acc_ref)

    i0 = pl.multiple_of(ig * IG_PEER, IG_PEER)
    for il in range(IG_PEER):
        a_ref[...] = jnp.dot(u_ref[il * PEER_NKEYS:(il + 1) * PEER_NKEYS, :], h2T_ref[...],
                             preferred_element_type=F32)
        for c in range(tm // LANES):
            sl = slice(c * LANES, (c + 1) * LANES)
            w = jnp.zeros((PEER_NKEYS, LANES), F32)
            for h in range(PEER_HEADS):
                th = th_ref[h, pl.ds(i0, IG_PEER), sl][il:il + 1, :]
                p0 = p0_ref[h, pl.ds(i0, IG_PEER), sl][il:il + 1, :]
                w = w + jnp.where(s1_ref[h, :, sl] >= th, p1_ref[h, :, sl] * p0, 0.0)
            a = a_ref[:, sl]
            gelu = 0.5 * a * (1.0 + lax.erf(a * (2.0 ** -0.5)))
            g_ref[:, sl] = (w * gelu).astype(BF16)
        acc_ref[...] += jnp.dot(vT_ref[:, il * PEER_NKEYS:(il + 1) * PEER_NKEYS], g_ref[...],
                                preferred_element_type=F32)

    @pl.when(ig == pl.num_programs(1) - 1)
    def _():
        y = x1_ref[...] + acc_ref[...].T
        ms = jnp.mean(y * y, axis=-1, keepdims=True)
        out_ref[...] = y * lax.rsqrt(ms + EPS) * fn_ref[...]


def _peer_dense(h2T, s1, p1, th, p0, peer_u, peer_v, x1, final_norm):
    D, N = h2T.shape
    tm = TM_PEER
    n_exp = peer_u.shape[0]
    u = peer_u.astype(BF16)
    vT = peer_v.T.astype(BF16)
    rows = IG_PEER * PEER_NKEYS
    rt = pl.BlockSpec((PEER_HEADS, PEER_NKEYS, tm), lambda t, g: (0, 0, t))
    return pl.pallas_call(
        _peer_dense_kernel,
        grid=(N // tm, n_exp // rows),
        in_specs=[pl.BlockSpec((D, tm), lambda t, g: (0, t)), rt, rt, rt, rt,
                  pl.BlockSpec((rows, D), lambda t, g: (g, 0)),
                  pl.BlockSpec((D, rows), lambda t, g: (0, g)),
                  pl.BlockSpec((tm, D), lambda t, g: (t, 0)),
                  pl.BlockSpec((1, D), lambda t, g: (0, 0))],
        out_specs=pl.BlockSpec((tm, D), lambda t, g: (t, 0)),
        out_shape=jax.ShapeDtypeStruct((N, D), F32),
        scratch_shapes=[pltpu.VMEM((D, tm), F32), pltpu.VMEM((PEER_NKEYS, tm), F32),
                        pltpu.VMEM((PEER_NKEYS, tm), BF16)],
        compiler_params=_params(("parallel", "arbitrary")),
        name="peer_dense",
    )(h2T, s1, p1, th, p0, u, vT, x1, final_norm.reshape(1, D))


def kernel(x, attn_norm, w_in, ret_gn, w_ret_o, w_dsa_o, w_out, ffn_norm, peer_wq, peer_subkeys, peer_u, peer_v,
           final_norm):
    B, L, D = x.shape
    assert attn_norm.shape[0] == 1, "single-layer block"
    assert L % KB == 0 and L % TM_PEER == 0 and D % LANES == 0
    rq, rk, rv, rg, dq, dkT, dv, iq, ikT, iw, gates = _inproj(x, attn_norm[0], w_in[0])
    y_ret = _retention(rq, rk, rv, rg, ret_gn[0])
    y_dsa = _dsa(iq, iw, ikT, dq, dkT, dv)
    x1, h2T = _merge(y_ret.reshape(B * L, RET_V), y_dsa.reshape(B * L, DSA_W), gates.reshape(B * L, 2 * D),
                     x.reshape(B * L, D), w_ret_o[0], w_dsa_o[0], w_out[0], ffn_norm[0])
    s1, p1, th, p0 = _peer_route(h2T, peer_wq[0], peer_subkeys[0])
    out = _peer_dense(h2T, s1, p1, th, p0, peer_u[0], peer_v[0], x1, final_norm)
    return out.reshape(B, L, D)
```

```python
import functools

import jax
import jax.numpy as jnp
import numpy as np
from jax import lax
from jax.experimental import pallas as pl
from jax.experimental.pallas import tpu as pltpu

F32 = jnp.float32
BF16 = jnp.bfloat16
I32 = jnp.int32

CHUNK = 64
EPS = 1e-6
RET_HEADS, RET_DK, RET_DV, RET_THETA = 4, 64, 128, 10000.0
DSA_HEADS, DSA_DH, IDX_HEADS, IDX_DH = 8, 64, 4, 64
DSA_TOPK_MAX = 256
ROPE_THETA, ROPE_DIM = 500000.0, 16
PEER_HEADS, PEER_NKEYS, PEER_DKEY, PEER_TOPK = 8, 128, 256, 16

LANES = 128
VMEM_LIMIT = 56 * 1024 * 1024

RET_QK = RET_HEADS * RET_DK
RET_V = RET_HEADS * RET_DV
DSA_W = DSA_HEADS * DSA_DH
IDX_Q = IDX_HEADS * IDX_DH

TM_IN = 512
TB_RET = 256
QB = 128
KB = 512
TM_MERGE = 512
TM_ROUTE = 256
TM_PEER = 512
IG_PEER = 16

NEG_BIG = -1e30
M_INIT = -1e20
INT_MIN = -(2 ** 31)

_OFF = {}
_o = 0
for _name, _w in (("rq", RET_QK), ("rk", RET_QK), ("rv", RET_V), ("rg", RET_V), ("dq", DSA_W), ("dk", DSA_W),
                  ("dv", DSA_W), ("iq", IDX_Q), ("ik", LANES), ("iw", LANES), ("gates", None)):
    _OFF[_name] = _o
    if _w is not None:
        _o += _w


def _params(sem):
    return pltpu.CompilerParams(dimension_semantics=sem, vmem_limit_bytes=VMEM_LIMIT)


def _rope_chunk(p, cos, sin, half):
    lane = lax.broadcasted_iota(I32, p.shape, 1) % 64
    partner = jnp.where(lane < half, pltpu.roll(p, LANES - half, 1), pltpu.roll(p, half, 1))
    return p * cos + partner * sin


def _inproj_kernel(x_ref, g_ref, w_ref, cr_ref, sr_ref, cp_ref, sp_ref,
                   rq_ref, rk_ref, rv_ref, rg_ref, dq_ref, dkT_ref, dv_ref, iq_ref, ikT_ref, iw_ref, gate_ref,
                   *, d_model):
    x = x_ref[...]
    ms = jnp.mean(x * x, axis=-1, keepdims=True)
    h = (x * lax.rsqrt(ms + EPS) * g_ref[...]).astype(BF16)

    def proj(off, width):
        return jnp.dot(h, w_ref[:, off:off + width], preferred_element_type=F32)

    cr, sr, cp, sp = cr_ref[...], sr_ref[...], cp_ref[...], sp_ref[...]
    qk_scale = RET_DK ** -0.5
    for c in range(RET_QK // LANES):
        sl = slice(c * LANES, (c + 1) * LANES)
        rq_ref[:, sl] = _rope_chunk(proj(_OFF["rq"] + c * LANES, LANES), cr, sr, RET_DK // 2) * qk_scale
        rk_ref[:, sl] = _rope_chunk(proj(_OFF["rk"] + c * LANES, LANES), cr, sr, RET_DK // 2)
    rv_ref[...] = proj(_OFF["rv"], RET_V).astype(BF16)
    rg_ref[...] = proj(_OFF["rg"], RET_V)
    dh_scale = DSA_DH ** -0.5
    for c in range(DSA_W // LANES):
        sl = slice(c * LANES, (c + 1) * LANES)
        dq_ref[:, sl] = (_rope_chunk(proj(_OFF["dq"] + c * LANES, LANES), cp, sp, ROPE_DIM // 2) * dh_scale).astype(BF16)
        dk = _rope_chunk(proj(_OFF["dk"] + c * LANES, LANES), cp, sp, ROPE_DIM // 2)
        dkT_ref[sl, :] = dk.T.astype(BF16)
    dv_ref[...] = proj(_OFF["dv"], DSA_W).astype(BF16)
    for c in range(IDX_Q // LANES):
        sl = slice(c * LANES, (c + 1) * LANES)
        iq_ref[:, sl] = _rope_chunk(proj(_OFF["iq"] + c * LANES, LANES), cp, sp, ROPE_DIM // 2).astype(BF16)
    ik = _rope_chunk(proj(_OFF["ik"], LANES), cp, sp, ROPE_DIM // 2)
    ikT_ref[...] = ik.T[:IDX_DH, :].astype(BF16)
    iw_ref[...] = proj(_OFF["iw"], LANES) * ((IDX_HEADS ** -0.5) * (IDX_DH ** -0.5))
    gate_ref[...] = proj(_OFF["gates"], 2 * d_model)


def _rope_tables(L, half, theta, pass_dims):
    inv = theta ** (-jnp.arange(half, dtype=F32) / half)
    ang = jnp.arange(L).astype(F32)[:, None] * inv[None, :]
    cos, sin = jnp.cos(ang), jnp.sin(ang)
    ones = jnp.ones((L, pass_dims), F32)
    zeros = jnp.zeros((L, pass_dims), F32)
    cos64 = jnp.concatenate([cos, cos, ones], axis=-1)
    sin64 = jnp.concatenate([-sin, sin, zeros], axis=-1)
    return jnp.tile(cos64, (1, 2)), jnp.tile(sin64, (1, 2))


def _inproj(x, attn_norm, w_in):
    B, L, D = x.shape
    tm = TM_IN
    nl = L // tm
    sizes = (RET_QK, RET_QK, RET_V, RET_V, DSA_W, DSA_W, DSA_W, IDX_Q, IDX_DH, IDX_HEADS, 2 * D)
    offs = np.concatenate([[0], np.cumsum(sizes)])
    cols = [w_in[:, offs[i]:offs[i + 1]] for i in range(len(sizes))]
    cols[8] = jnp.pad(cols[8], ((0, 0), (0, LANES - IDX_DH)))
    cols[9] = jnp.pad(cols[9], ((0, 0), (0, LANES - IDX_HEADS)))
    w = jnp.concatenate(cols, axis=1).astype(BF16)
    wtot = w.shape[1]
    cr, sr = _rope_tables(L, RET_DK // 2, RET_THETA, 0)
    cp, sp = _rope_tables(L, ROPE_DIM // 2, ROPE_THETA, DSA_DH - ROPE_DIM)

    tok = lambda width: pl.BlockSpec((None, tm, width), lambda b, l: (b, l, 0))
    tab = pl.BlockSpec((tm, LANES), lambda b, l: (l, 0))
    out_shape = (
        jax.ShapeDtypeStruct((B, L, RET_QK), F32),
        jax.ShapeDtypeStruct((B, L, RET_QK), F32),
        jax.ShapeDtypeStruct((B, L, RET_V), BF16),
        jax.ShapeDtypeStruct((B, L, RET_V), F32),
        jax.ShapeDtypeStruct((B, L, DSA_W), BF16),
        jax.ShapeDtypeStruct((B, nl, DSA_W, tm), BF16),
        jax.ShapeDtypeStruct((B, L, DSA_W), BF16),
        jax.ShapeDtypeStruct((B, L, IDX_Q), BF16),
        jax.ShapeDtypeStruct((B, nl, IDX_DH, tm), BF16),
        jax.ShapeDtypeStruct((B, L, LANES), F32),
        jax.ShapeDtypeStruct((B, L, 2 * D), F32),
    )
    out_specs = (
        tok(RET_QK), tok(RET_QK), tok(RET_V), tok(RET_V), tok(DSA_W),
        pl.BlockSpec((None, None, DSA_W, tm), lambda b, l: (b, l, 0, 0)),
        tok(DSA_W), tok(IDX_Q),
        pl.BlockSpec((None, None, IDX_DH, tm), lambda b, l: (b, l, 0, 0)),
        tok(LANES), tok(2 * D),
    )
    return pl.pallas_call(
        functools.partial(_inproj_kernel, d_model=D),
        grid=(B, nl),
        in_specs=[tok(D), pl.BlockSpec((1, D), lambda b, l: (0, 0)),
                  pl.BlockSpec((D, wtot), lambda b, l: (0, 0)), tab, tab, tab, tab],
        out_specs=out_specs,
        out_shape=out_shape,
        compiler_params=_params(("parallel", "parallel")),
        name="inproj",
    )(x, attn_norm.reshape(1, D), w, cr, sr, cp, sp)


def _retention_kernel(q_ref, k_ref, v_ref, g_ref, gn_ref, dm_ref, qd_ref, kd_ref, cd_ref, y_ref, state_ref):
    @pl.when(pl.program_id(1) == 0)
    def _():
        state_ref[...] = jnp.zeros_like(state_ref)

    for h in range(RET_HEADS):
        q = q_ref[:, h * RET_DK:(h + 1) * RET_DK]
        k = k_ref[:, h * RET_DK:(h + 1) * RET_DK]
        v = v_ref[:, h * RET_DV:(h + 1) * RET_DV]
        s = lax.dot_general(q.astype(BF16), k.astype(BF16), (((1,), (1,)), ((), ())), preferred_element_type=F32)
        intra = jnp.dot((s * dm_ref[h]).astype(BF16), v, preferred_element_type=F32)
        state = state_ref[h]
        cross = jnp.dot((q * qd_ref[h]).astype(BF16), state.astype(BF16), preferred_element_type=F32)
        kv = lax.dot_general((k * kd_ref[h]).astype(BF16), v, (((0,), (0,)), ((), ())), preferred_element_type=F32)
        state_ref[h] = cd_ref[h] * state + kv
        y = intra + cross
        mu = jnp.mean(y, axis=-1, keepdims=True)
        yc = y - mu
        var = jnp.mean(yc * yc, axis=-1, keepdims=True)
        sl = slice(h * RET_DV, (h + 1) * RET_DV)
        yn = yc * lax.rsqrt(var + EPS) * gn_ref[:, sl]
        g = g_ref[:, sl]
        y_ref[:, sl] = (g * jax.nn.sigmoid(g) * yn).astype(y_ref.dtype)


def _retention(rq, rk, rv, rg, ret_gn):
    B, L, _ = rq.shape
    tb = TB_RET
    log_g = jnp.log(1.0 - 2.0 ** (-5.0 - jnp.arange(RET_HEADS, dtype=F32)))
    i = jnp.arange(tb, dtype=F32)
    ci = jnp.arange(tb) // CHUNK
    visible = (ci[None, :] <= ci[:, None]).astype(F32)
    dm = jnp.exp(log_g[:, None, None] * jnp.abs(i[:, None] - i[None, :])) * visible[None]
    ones = jnp.ones((1, 1, RET_DK), F32)
    qd = jnp.exp(log_g[:, None] * (i + 1.0)[None, :])[:, :, None] * ones
    kd = jnp.exp(log_g[:, None] * (tb - 1.0 - i)[None, :])[:, :, None] * ones
    cd = jnp.exp(log_g * tb)[:, None, None] * jnp.ones((1, RET_DK, RET_DV), F32)

    tok = lambda width: pl.BlockSpec((None, tb, width), lambda b, l: (b, l, 0))
    full = lambda shape: pl.BlockSpec(shape, lambda b, l: (0,) * len(shape))
    return pl.pallas_call(
        _retention_kernel,
        grid=(B, L // tb),
        in_specs=[tok(RET_QK), tok(RET_QK), tok(RET_V), tok(RET_V), full((1, RET_V)),
                  full((RET_HEADS, tb, tb)), full((RET_HEADS, tb, RET_DK)), full((RET_HEADS, tb, RET_DK)),
                  full((RET_HEADS, RET_DK, RET_DV))],
        out_specs=tok(RET_V),
        out_shape=jax.ShapeDtypeStruct((B, L, RET_V), BF16),
        scratch_shapes=[pltpu.VMEM((RET_HEADS, RET_DK, RET_DV), F32)],
        compiler_params=_params(("parallel", "arbitrary")),
        name="retention",
    )(rq, rk, rv, rg, ret_gn.reshape(1, RET_V), dm, qd, kd, cd)


def _dsa_kernel(iq_ref, iw_ref, ikT_ref, dq_ref, dkT_ref, dv_ref, tri_ref, y_ref,
                keys_ref, qm_ref, m_ref, l_ref, acc_ref, *, topk):
    qb = pl.program_id(1)
    nkb = (qb * QB) // KB + 1
    n_sub = KB // LANES

    row = lax.broadcasted_iota(I32, (QB, KB), 0)
    col = lax.broadcasted_iota(I32, (QB, KB), 1)
    q_chunk = (qb * QB + row) // CHUNK

    iq = iq_ref[...]
    iw = iw_ref[...]

    def score_block(kb, carry):
        ik = ikT_ref[kb]
        acc = jnp.zeros((QB, KB), F32)
        for h in range(IDX_HEADS):
            r = jnp.dot(iq[:, h * IDX_DH:(h + 1) * IDX_DH], ik, preferred_element_type=F32)
            acc = acc + iw[:, h:h + 1] * jnp.maximum(r, 0.0)
        acc = acc + 0.0
        bits = pltpu.bitcast(acc, I32)
        key = jnp.where(bits < 0, bits ^ 0x7FFFFFFF, bits)
        admissible = (kb * KB + col) // CHUNK <= q_chunk
        keys_ref[kb] = jnp.where(admissible, key, INT_MIN)
        return carry

    lax.fori_loop(0, nkb, score_block, 0)

    def count_ge(cand):
        cand_b = jnp.broadcast_to(cand, (QB, LANES))

        def body(kb, cnt):
            for c in range(n_sub):
                kc = keys_ref[kb, :, c * LANES:(c + 1) * LANES]
                cnt = cnt + jnp.where(kc >= cand_b, 1, 0)
            return cnt

        cnt = lax.fori_loop(0, nkb, body, jnp.zeros((QB, LANES), I32))
        return jnp.sum(cnt, axis=1, keepdims=True)

    zero = jnp.zeros((QB, 1), I32)
    t0 = jnp.where(count_ge(zero) >= topk, zero, zero + INT_MIN)

    def bit_step(n, t):
        cand = t + jnp.left_shift(jnp.int32(1), 30 - n)
        return jnp.where(count_ge(cand) >= topk, cand, t)

    thr = lax.fori_loop(0, 31, bit_step, t0)
    n_gt = count_ge(thr + 1)
    need = jnp.where(thr == INT_MIN, 0, topk - n_gt).astype(F32)

    lane = lax.broadcasted_iota(I32, (QB, LANES), 1)
    for h in range(DSA_HEADS):
        qp = dq_ref[:, (h // 2) * LANES:(h // 2 + 1) * LANES]
        own = (lane < DSA_DH) if h % 2 == 0 else (lane >= DSA_DH)
        qm_ref[h] = jnp.where(own, qp, jnp.zeros_like(qp))
    m_ref[...] = jnp.full(m_ref.shape, M_INIT, F32)
    l_ref[...] = jnp.zeros(l_ref.shape, F32)
    acc_ref[...] = jnp.zeros(acc_ref.shape, F32)

    def attend_block(kb, n_eq_before):
        key = keys_ref[kb]
        eq = key == thr
        eq_f = jnp.where(eq, 1.0, 0.0)
        prefix = jnp.dot(eq_f.astype(BF16), tri_ref[...], preferred_element_type=F32)
        keep_tie = (n_eq_before + prefix) < need
        bias = jnp.where(key > thr, 0.0, jnp.where(eq, jnp.where(keep_tie, 0.0, NEG_BIG), NEG_BIG))
        v_blk = dv_ref[pl.ds(pl.multiple_of(kb * KB, KB), KB), :]
        for h in range(DSA_HEADS):
            pair = h // 2
            kT = dkT_ref[kb, pair * LANES:(pair + 1) * LANES, :]
            s = jnp.dot(qm_ref[h], kT, preferred_element_type=F32) + bias
            m_prev = m_ref[h]
            m_new = jnp.maximum(m_prev, jnp.max(s, axis=1, keepdims=True))
            p = jnp.exp(s - m_new)
            alpha = jnp.exp(m_prev - m_new)
            l_ref[h] = alpha * l_ref[h] + jnp.sum(p, axis=1, keepdims=True)
            m_ref[h] = m_new
            pv = jnp.dot(p.astype(BF16), v_blk[:, pair * LANES:(pair + 1) * LANES], preferred_element_type=F32)
            acc_ref[h] = alpha * acc_ref[h] + pv
        return n_eq_before + jnp.sum(eq_f, axis=1, keepdims=True)

    lax.fori_loop(0, nkb, attend_block, jnp.zeros((QB, 1), F32))

    for pair in range(DSA_HEADS // 2):
        even = acc_ref[2 * pair] / l_ref[2 * pair]
        odd = acc_ref[2 * pair + 1] / l_ref[2 * pair + 1]
        y_ref[:, pair * LANES:(pair + 1) * LANES] = jnp.where(lane < DSA_DH, even, odd).astype(y_ref.dtype)


def _dsa(iq, iw, ikT, dq, dkT, dv):
    B, L, _ = dq.shape
    nkb_all = L // KB
    topk = min(DSA_TOPK_MAX, L // 4)
    r = jnp.arange(KB)
    tri = (r[:, None] < r[None, :]).astype(BF16)
    qtok = lambda width: pl.BlockSpec((None, QB, width), lambda b, q: (b, q, 0))
    return pl.pallas_call(
        functools.partial(_dsa_kernel, topk=topk),
        grid=(B, L // QB),
        in_specs=[qtok(IDX_Q), qtok(LANES),
                  pl.BlockSpec((None, nkb_all, IDX_DH, KB), lambda b, q: (b, 0, 0, 0)),
                  qtok(DSA_W),
                  pl.BlockSpec((None, nkb_all, DSA_W, KB), lambda b, q: (b, 0, 0, 0)),
                  pl.BlockSpec((None, L, DSA_W), lambda b, q: (b, 0, 0)),
                  pl.BlockSpec((KB, KB), lambda b, q: (0, 0))],
        out_specs=qtok(DSA_W),
        out_shape=jax.ShapeDtypeStruct((B, L, DSA_W), BF16),
        scratch_shapes=[pltpu.VMEM((nkb_all, QB, KB), I32),
                        pltpu.VMEM((DSA_HEADS, QB, LANES), BF16),
                        pltpu.VMEM((DSA_HEADS, QB, 1), F32),
                        pltpu.VMEM((DSA_HEADS, QB, 1), F32),
                        pltpu.VMEM((DSA_HEADS, QB, LANES), F32)],
        compiler_params=_params(("parallel", "arbitrary")),
        name="dsa",
    )(iq, iw, ikT, dq, dkT, dv, tri)


def _merge_kernel(yr_ref, yd_ref, gate_ref, x_ref, wr_ref, wd_ref, wo_ref, fn_ref, x1_ref, h2T_ref, *, d_model):
    g = gate_ref[...]
    br = jnp.dot(yr_ref[...], wr_ref[...], preferred_element_type=F32)
    bd = jnp.dot(yd_ref[...], wd_ref[...], preferred_element_type=F32)
    merged = jax.nn.sigmoid(g[:, :d_model]) * br + jax.nn.sigmoid(g[:, d_model:]) * bd
    x1 = x_ref[...] + jnp.dot(merged.astype(BF16), wo_ref[...], preferred_element_type=F32)
    x1_ref[...] = x1
    ms = jnp.mean(x1 * x1, axis=-1, keepdims=True)
    h2 = x1 * lax.rsqrt(ms + EPS) * fn_ref[...]
    h2T_ref[...] = h2.T.astype(BF16)


def _merge(y_ret, y_dsa, gates, x, w_ret_o, w_dsa_o, w_out, ffn_norm):
    N, D = x.shape
    tm = TM_MERGE
    tok = lambda width: pl.BlockSpec((tm, width), lambda t: (t, 0))
    full = lambda shape: pl.BlockSpec(shape, lambda t: (0, 0))
    return pl.pallas_call(
        functools.partial(_merge_kernel, d_model=D),
        grid=(N // tm,),
        in_specs=[tok(RET_V), tok(DSA_W), tok(2 * D), tok(D),
                  full((RET_V, D)), full((DSA_W, D)), full((D, D)), full((1, D))],
        out_specs=(tok(D), pl.BlockSpec((D, tm), lambda t: (0, t))),
        out_shape=(jax.ShapeDtypeStruct((N, D), F32), jax.ShapeDtypeStruct((D, N), BF16)),
        compiler_params=_params(("parallel",)),
        name="merge",
    )(y_ret, y_dsa, gates, x, w_ret_o.astype(BF16), w_dsa_o.astype(BF16), w_out.astype(BF16),
      ffn_norm.reshape(1, D))


_ROW_LEN = [PEER_TOPK // (a + 1) for a in range(PEER_TOPK)]


def _extract_top(s, n):
    rows = lax.broadcasted_iota(I32, s.shape, 0)
    rank = jnp.full(s.shape, float(n), F32)
    vals = []
    for a in range(n):
        m = jnp.max(s, axis=0, keepdims=True)
        first = jnp.min(jnp.where(s == m, rows, s.shape[0]), axis=0, keepdims=True)
        hit = rows == first
        s = jnp.where(hit, -jnp.inf, s)
        rank = jnp.where(hit, float(a), rank)
        vals.append(m)
    return vals, rank


def _peer_route_kernel(h2T_ref, wqT_ref, sk_ref, r1_ref, p1_ref, n_ref, p0_ref):
    h2T = h2T_ref[...]
    K = PEER_TOPK
    for h in range(PEER_HEADS):
        sT = []
        for c in range(2):
            r0 = (h * 2 + c) * PEER_NKEYS
            qT = jnp.dot(wqT_ref[r0:r0 + PEER_NKEYS, :], h2T, preferred_element_type=F32)
            sT.append(jnp.dot(sk_ref[h, c], qT.astype(BF16), preferred_element_type=F32))
        v0, rank0 = _extract_top(sT[0], K)
        v1, rank1 = _extract_top(sT[1], K)
        cand = [[v0[a] + v1[b] for b in range(_ROW_LEN[a])] for a in range(K)]
        best, _ = _extract_top(jnp.concatenate([c for row in cand for c in row], axis=0), K)
        tau = best[K - 1]
        z = jnp.zeros_like(tau)
        for k in range(K):
            z = z + jnp.exp(best[k] - best[0])
        gt = [sum(jnp.where(c > tau, 1.0, 0.0) for c in row) for row in cand]
        eq = [sum(jnp.where(c == tau, 1.0, 0.0) for c in row) for row in cand]
        need = float(K) - sum(gt)
        n = jnp.zeros(rank0.shape, F32)
        for a in range(K):
            n_a = gt[a] + jnp.minimum(jnp.maximum(need, 0.0), eq[a])
            need = need - eq[a]
            n = jnp.where(rank0 == float(a), n_a, n)
        r1_ref[h] = rank1.astype(BF16)
        p1_ref[h] = jnp.exp(sT[1] - v1[0]).astype(BF16)
        n_ref[h] = n.astype(BF16)
        p0_ref[h] = (jnp.exp(sT[0] - v0[0]) / z).astype(BF16)


def _peer_route(h2T, peer_wq, peer_subkeys):
    D, N = h2T.shape
    tm = TM_ROUTE
    wqT = peer_wq.T.astype(BF16)
    sk = peer_subkeys.astype(BF16)
    spec = pl.BlockSpec((PEER_HEADS, PEER_NKEYS, tm), lambda t: (0, 0, t))
    shape = lambda dt: jax.ShapeDtypeStruct((PEER_HEADS, PEER_NKEYS, N), dt)
    return pl.pallas_call(
        _peer_route_kernel,
        grid=(N // tm,),
        in_specs=[pl.BlockSpec((D, tm), lambda t: (0, t)),
                  pl.BlockSpec(wqT.shape, lambda t: (0, 0)),
                  pl.BlockSpec(sk.shape, lambda t: (0, 0, 0, 0))],
        out_specs=(spec, spec, spec, spec),
        out_shape=(shape(BF16), shape(BF16), shape(BF16), shape(BF16)),
        compiler_params=_params(("parallel",)),
        name="peer_route",
    )(h2T, wqT, sk)


def _peer_dense_kernel(h2T_ref, r1_ref, p1_ref, n_ref, p0_ref, u_ref, vT_ref, x1_ref, fn_ref, out_ref,
                       acc_ref, a_ref, g_ref, r1s_ref, p1s_ref, *, n_groups):
    step = pl.program_id(1)
    tm = h2T_ref.shape[1]

    def project(slot):
        a_ref[slot] = jnp.dot(u_ref[...], h2T_ref[...], preferred_element_type=F32)

    def gate(slot):
        for c in range(tm // LANES):
            sl = slice(c * LANES, (c + 1) * LANES)
            for il in range(IG_PEER):
                rows = slice(il * PEER_NKEYS, (il + 1) * PEER_NKEYS)
                w = jnp.zeros((PEER_NKEYS, LANES), BF16)
                for h in range(PEER_HEADS):
                    n_i = n_ref[h, il:il + 1, sl]
                    p0 = p0_ref[h, il:il + 1, sl]
                    w = w + jnp.where(r1s_ref[h, :, sl] < n_i, p1s_ref[h, :, sl] * p0, jnp.zeros_like(w))
                a = a_ref[slot, rows, sl]
                gelu = 0.5 * a * (1.0 + lax.erf(a * (2.0 ** -0.5)))
                g_ref[slot, rows, sl] = (w.astype(F32) * gelu).astype(BF16)

    def combine(slot):
        acc_ref[...] += jnp.dot(vT_ref[...], g_ref[slot], preferred_element_type=F32)

    @pl.when(step == 0)
    def _():
        acc_ref[...] = jnp.zeros_like(acc_ref)
        g_ref[1] = jnp.zeros(g_ref.shape[1:], BF16)
        r1s_ref[...] = r1_ref[...]
        p1s_ref[...] = p1_ref[...]
        project(0)

    @pl.when(jnp.logical_and(step > 0, step <= n_groups))
    def _():
        project(step % 2)
        gate((step - 1) % 2)
        combine(step % 2)

    @pl.when(step == n_groups + 1)
    def _():
        combine(step % 2)
        y = x1_ref[...] + acc_ref[...].T
        ms = jnp.mean(y * y, axis=-1, keepdims=True)
        out_ref[...] = y * lax.rsqrt(ms + EPS) * fn_ref[...]


def _peer_dense(h2T, r1, p1, n, p0, peer_u, peer_v, x1, final_norm):
    D, N = h2T.shape
    tm = TM_PEER
    n_exp = peer_u.shape[0]
    u = peer_u.astype(BF16)
    vT = peer_v.T.astype(BF16)
    rows = IG_PEER * PEER_NKEYS
    ng = n_exp // rows
    rt = pl.BlockSpec((PEER_HEADS, PEER_NKEYS, tm), lambda t, s: (0, 0, t))
    grp = pl.BlockSpec((PEER_HEADS, None, IG_PEER, tm), lambda t, s: (0, jnp.clip(s - 1, 0, ng - 1), 0, t))
    n = n.reshape(PEER_HEADS, ng, IG_PEER, N)
    p0 = p0.reshape(PEER_HEADS, ng, IG_PEER, N)
    return pl.pallas_call(
        functools.partial(_peer_dense_kernel, n_groups=ng),
        grid=(N // tm, ng + 2),
        in_specs=[pl.BlockSpec((D, tm), lambda t, s: (0, t)), rt, rt, grp, grp,
                  pl.BlockSpec((rows, D), lambda t, s: (jnp.minimum(s, ng - 1), 0)),
                  pl.BlockSpec((D, rows), lambda t, s: (0, jnp.clip(s - 2, 0, ng - 1))),
                  pl.BlockSpec((tm, D), lambda t, s: (t, 0)),
                  pl.BlockSpec((1, D), lambda t, s: (0, 0))],
        out_specs=pl.BlockSpec((tm, D), lambda t, s: (t, 0)),
        out_shape=jax.ShapeDtypeStruct((N, D), F32),
        scratch_shapes=[pltpu.VMEM((D, tm), F32), pltpu.VMEM((2, rows, tm), F32), pltpu.VMEM((2, rows, tm), BF16),
                        pltpu.VMEM((PEER_HEADS, PEER_NKEYS, tm), BF16), pltpu.VMEM((PEER_HEADS, PEER_NKEYS, tm), BF16)],
        compiler_params=_params(("parallel", "arbitrary")),
        name="peer_dense",
    )(h2T, r1, p1, n, p0, u, vT, x1, final_norm.reshape(1, D))


def kernel(x, attn_norm, w_in, ret_gn, w_ret_o, w_dsa_o, w_out, ffn_norm, peer_wq, peer_subkeys, peer_u, peer_v,
           final_norm):
    B, L, D = x.shape
    assert attn_norm.shape[0] == 1, "single-layer block"
    assert L % KB == 0 and L % TM_PEER == 0 and D % LANES == 0
    rq, rk, rv, rg, dq, dkT, dv, iq, ikT, iw, gates = _inproj(x, attn_norm[0], w_in[0])
    y_ret = _retention(rq, rk, rv, rg, ret_gn[0])
    y_dsa = _dsa(iq, iw, ikT, dq, dkT, dv)
    x1, h2T = _merge(y_ret.reshape(B * L, RET_V), y_dsa.reshape(B * L, DSA_W), gates.reshape(B * L, 2 * D),
                     x.reshape(B * L, D), w_ret_o[0], w_dsa_o[0], w_out[0], ffn_norm[0])
    r1, p1, n, p0 = _peer_route(h2T, peer_wq[0], peer_subkeys[0])
    out = _peer_dense(h2T, r1, p1, n, p0, peer_u[0], peer_v[0], x1, final_norm)
    return out.reshape(B, L, D)
```

```python
import functools

import jax
import jax.numpy as jnp
import numpy as np
from jax import lax
from jax.experimental import pallas as pl
from jax.experimental.pallas import tpu as pltpu

F32 = jnp.float32
BF16 = jnp.bfloat16
I32 = jnp.int32

CHUNK = 64
EPS = 1e-6
RET_HEADS, RET_DK, RET_DV, RET_THETA = 4, 64, 128, 10000.0
DSA_HEADS, DSA_DH, IDX_HEADS, IDX_DH = 8, 64, 4, 64
DSA_TOPK_MAX = 256
ROPE_THETA, ROPE_DIM = 500000.0, 16
PEER_HEADS, PEER_NKEYS, PEER_DKEY, PEER_TOPK = 8, 128, 256, 16

LANES = 128
VMEM_LIMIT = 56 * 1024 * 1024

RET_QK = RET_HEADS * RET_DK
RET_V = RET_HEADS * RET_DV
DSA_W = DSA_HEADS * DSA_DH
IDX_Q = IDX_HEADS * IDX_DH

TM_IN = 512
TB_RET = 256
QB = 128
KB = 256
SM_ROWS = 64
TM_MERGE = 512
TM_ROUTE = 256
TM_PEER = 1024
IG_PEER = 4

NEG_BIG = -1e30
M_INIT = -1e20
INT_MIN = -(2 ** 31)

_OFF = {}
_o = 0
for _name, _w in (("rq", RET_QK), ("rk", RET_QK), ("rv", RET_V), ("rg", RET_V), ("dq", DSA_W), ("dk", DSA_W),
                  ("dv", DSA_W), ("iq", IDX_Q), ("ik", LANES), ("iw", LANES), ("gates", None)):
    _OFF[_name] = _o
    if _w is not None:
        _o += _w


def _params(sem):
    return pltpu.CompilerParams(dimension_semantics=sem, vmem_limit_bytes=VMEM_LIMIT)


def _rope_chunk(p, cos, sin, half):
    lane = lax.broadcasted_iota(I32, p.shape, 1) % 64
    partner = jnp.where(lane < half, pltpu.roll(p, LANES - half, 1), pltpu.roll(p, half, 1))
    return p * cos + partner * sin


def _inproj_kernel(x_ref, g_ref, w_ref, cr_ref, sr_ref, cp_ref, sp_ref,
                   rq_ref, rk_ref, rv_ref, rg_ref, dqT_ref, dk_ref, dvT_ref, iqT_ref, ik_ref, iwT_ref, gate_ref,
                   *, d_model):
    x = x_ref[...]
    ms = jnp.mean(x * x, axis=-1, keepdims=True)
    h = (x * lax.rsqrt(ms + EPS) * g_ref[...]).astype(BF16)

    def proj(off, width):
        return jnp.dot(h, w_ref[:, off:off + width], preferred_element_type=F32)

    cr, sr, cp, sp = cr_ref[...], sr_ref[...], cp_ref[...], sp_ref[...]
    qk_scale = RET_DK ** -0.5
    for c in range(RET_QK // LANES):
        sl = slice(c * LANES, (c + 1) * LANES)
        rq_ref[:, sl] = _rope_chunk(proj(_OFF["rq"] + c * LANES, LANES), cr, sr, RET_DK // 2) * qk_scale
        rk_ref[:, sl] = _rope_chunk(proj(_OFF["rk"] + c * LANES, LANES), cr, sr, RET_DK // 2)
    rv_ref[...] = proj(_OFF["rv"], RET_V).astype(BF16)
    rg_ref[...] = proj(_OFF["rg"], RET_V)
    tm = x.shape[0]
    dh_scale = DSA_DH ** -0.5
    for c in range(DSA_W // LANES):
        sl = slice(c * LANES, (c + 1) * LANES)
        dq = _rope_chunk(proj(_OFF["dq"] + c * LANES, LANES), cp, sp, ROPE_DIM // 2) * dh_scale
        dqT_ref[sl, :] = dq.T.astype(BF16)
        dk_ref[:, sl] = _rope_chunk(proj(_OFF["dk"] + c * LANES, LANES), cp, sp, ROPE_DIM // 2).astype(BF16)
        dv = proj(_OFF["dv"] + c * LANES, LANES)
        for j in range(tm // KB):
            dvT_ref[j, sl, :] = dv[j * KB:(j + 1) * KB, :].T.astype(BF16)
    for c in range(IDX_Q // LANES):
        iq = _rope_chunk(proj(_OFF["iq"] + c * LANES, LANES), cp, sp, ROPE_DIM // 2)
        for j in range(tm // QB):
            t = iq[j * QB:(j + 1) * QB, :].T.astype(BF16)
            iqT_ref[j, :, (2 * c) * QB:(2 * c + 1) * QB] = t[:IDX_DH, :]
            iqT_ref[j, :, (2 * c + 1) * QB:(2 * c + 2) * QB] = t[IDX_DH:, :]
    ik = _rope_chunk(proj(_OFF["ik"], LANES), cp, sp, ROPE_DIM // 2)
    ik_ref[...] = ik[:, :IDX_DH].astype(BF16)
    iw = proj(_OFF["iw"], LANES) * ((IDX_HEADS ** -0.5) * (IDX_DH ** -0.5))
    iwT_ref[...] = iw.T[:8, :]
    gate_ref[...] = proj(_OFF["gates"], 2 * d_model)


def _rope_tables(L, half, theta, pass_dims):
    inv = theta ** (-jnp.arange(half, dtype=F32) / half)
    ang = jnp.arange(L).astype(F32)[:, None] * inv[None, :]
    cos, sin = jnp.cos(ang), jnp.sin(ang)
    ones = jnp.ones((L, pass_dims), F32)
    zeros = jnp.zeros((L, pass_dims), F32)
    cos64 = jnp.concatenate([cos, cos, ones], axis=-1)
    sin64 = jnp.concatenate([-sin, sin, zeros], axis=-1)
    return jnp.tile(cos64, (1, 2)), jnp.tile(sin64, (1, 2))


def _inproj(x, attn_norm, w_in):
    B, L, D = x.shape
    tm = TM_IN
    nl = L // tm
    sizes = (RET_QK, RET_QK, RET_V, RET_V, DSA_W, DSA_W, DSA_W, IDX_Q, IDX_DH, IDX_HEADS, 2 * D)
    offs = np.concatenate([[0], np.cumsum(sizes)])
    cols = [w_in[:, offs[i]:offs[i + 1]] for i in range(len(sizes))]
    cols[8] = jnp.pad(cols[8], ((0, 0), (0, LANES - IDX_DH)))
    cols[9] = jnp.pad(cols[9], ((0, 0), (0, LANES - IDX_HEADS)))
    w = jnp.concatenate(cols, axis=1).astype(BF16)
    wtot = w.shape[1]
    cr, sr = _rope_tables(L, RET_DK // 2, RET_THETA, 0)
    cp, sp = _rope_tables(L, ROPE_DIM // 2, ROPE_THETA, DSA_DH - ROPE_DIM)

    tok = lambda width: pl.BlockSpec((None, tm, width), lambda b, l: (b, l, 0))
    tab = pl.BlockSpec((tm, LANES), lambda b, l: (l, 0))
    out_shape = (
        jax.ShapeDtypeStruct((B, L, RET_QK), F32),
        jax.ShapeDtypeStruct((B, L, RET_QK), F32),
        jax.ShapeDtypeStruct((B, L, RET_V), BF16),
        jax.ShapeDtypeStruct((B, L, RET_V), F32),
        jax.ShapeDtypeStruct((B, DSA_W, L), BF16),
        jax.ShapeDtypeStruct((B, L, DSA_W), BF16),
        jax.ShapeDtypeStruct((B, L // KB, DSA_W, KB), BF16),
        jax.ShapeDtypeStruct((B, L // QB, IDX_DH, IDX_HEADS * QB), BF16),
        jax.ShapeDtypeStruct((B, L, IDX_DH), BF16),
        jax.ShapeDtypeStruct((B, 8, L), F32),
        jax.ShapeDtypeStruct((B, L, 2 * D), F32),
    )
    out_specs = (
        tok(RET_QK), tok(RET_QK), tok(RET_V), tok(RET_V),
        pl.BlockSpec((None, DSA_W, tm), lambda b, l: (b, 0, l)),
        tok(DSA_W),
        pl.BlockSpec((None, tm // KB, DSA_W, KB), lambda b, l: (b, l, 0, 0)),
        pl.BlockSpec((None, tm // QB, IDX_DH, IDX_HEADS * QB), lambda b, l: (b, l, 0, 0)),
        tok(IDX_DH),
        pl.BlockSpec((None, 8, tm), lambda b, l: (b, 0, l)),
        tok(2 * D),
    )
    return pl.pallas_call(
        functools.partial(_inproj_kernel, d_model=D),
        grid=(B, nl),
        in_specs=[tok(D), pl.BlockSpec((1, D), lambda b, l: (0, 0)),
                  pl.BlockSpec((D, wtot), lambda b, l: (0, 0)), tab, tab, tab, tab],
        out_specs=out_specs,
        out_shape=out_shape,
        compiler_params=_params(("parallel", "parallel")),
        name="inproj",
    )(x, attn_norm.reshape(1, D), w, cr, sr, cp, sp)


def _retention_kernel(q_ref, k_ref, v_ref, g_ref, gn_ref, dm_ref, qd_ref, kd_ref, cd_ref, y_ref, state_ref):
    @pl.when(pl.program_id(1) == 0)
    def _():
        state_ref[...] = jnp.zeros_like(state_ref)

    for h in range(RET_HEADS):
        q = q_ref[:, h * RET_DK:(h + 1) * RET_DK]
        k = k_ref[:, h * RET_DK:(h + 1) * RET_DK]
        v = v_ref[:, h * RET_DV:(h + 1) * RET_DV]
        s = lax.dot_general(q.astype(BF16), k.astype(BF16), (((1,), (1,)), ((), ())), preferred_element_type=F32)
        intra = jnp.dot((s * dm_ref[h]).astype(BF16), v, preferred_element_type=F32)
        state = state_ref[h]
        cross = jnp.dot((q * qd_ref[h]).astype(BF16), state.astype(BF16), preferred_element_type=F32)
        kv = lax.dot_general((k * kd_ref[h]).astype(BF16), v, (((0,), (0,)), ((), ())), preferred_element_type=F32)
        state_ref[h] = cd_ref[h] * state + kv
        y = intra + cross
        mu = jnp.mean(y, axis=-1, keepdims=True)
        yc = y - mu
        var = jnp.mean(yc * yc, axis=-1, keepdims=True)
        sl = slice(h * RET_DV, (h + 1) * RET_DV)
        yn = yc * lax.rsqrt(var + EPS) * gn_ref[:, sl]
        g = g_ref[:, sl]
        y_ref[:, sl] = (g * jax.nn.sigmoid(g) * yn).astype(y_ref.dtype)


def _retention(rq, rk, rv, rg, ret_gn):
    B, L, _ = rq.shape
    tb = TB_RET
    log_g = jnp.log(1.0 - 2.0 ** (-5.0 - jnp.arange(RET_HEADS, dtype=F32)))
    i = jnp.arange(tb, dtype=F32)
    ci = jnp.arange(tb) // CHUNK
    visible = (ci[None, :] <= ci[:, None]).astype(F32)
    dm = jnp.exp(log_g[:, None, None] * jnp.abs(i[:, None] - i[None, :])) * visible[None]
    ones = jnp.ones((1, 1, RET_DK), F32)
    qd = jnp.exp(log_g[:, None] * (i + 1.0)[None, :])[:, :, None] * ones
    kd = jnp.exp(log_g[:, None] * (tb - 1.0 - i)[None, :])[:, :, None] * ones
    cd = jnp.exp(log_g * tb)[:, None, None] * jnp.ones((1, RET_DK, RET_DV), F32)

    tok = lambda width: pl.BlockSpec((None, tb, width), lambda b, l: (b, l, 0))
    full = lambda shape: pl.BlockSpec(shape, lambda b, l: (0,) * len(shape))
    return pl.pallas_call(
        _retention_kernel,
        grid=(B, L // tb),
        in_specs=[tok(RET_QK), tok(RET_QK), tok(RET_V), tok(RET_V), full((1, RET_V)),
                  full((RET_HEADS, tb, tb)), full((RET_HEADS, tb, RET_DK)), full((RET_HEADS, tb, RET_DK)),
                  full((RET_HEADS, RET_DK, RET_DV))],
        out_specs=tok(RET_V),
        out_shape=jax.ShapeDtypeStruct((B, L, RET_V), BF16),
        scratch_shapes=[pltpu.VMEM((RET_HEADS, RET_DK, RET_DV), F32)],
        compiler_params=_params(("parallel", "arbitrary")),
        name="retention",
    )(rq, rk, rv, rg, ret_gn.reshape(1, RET_V), dm, qd, kd, cd)


def _dsa_kernel(iqT_ref, iwT_ref, ik_ref, dqT_ref, dk_ref, dvT_ref, tri_ref, y_ref,
                keys_ref, rhs_ref, m_ref, l_ref, acc_ref, s_ref, p_ref, alpha_ref, *, topk):
    qb = pl.program_id(1)
    nkb = (qb * QB) // KB + 1
    n_trips = (nkb + 1) // 2
    n_pairs = DSA_HEADS // 2

    key_row = lax.broadcasted_iota(I32, (KB, QB), 0)
    q_chunk = (qb * QB + lax.broadcasted_iota(I32, (KB, QB), 1)) // CHUNK

    def score_block(kb, carry):
        k0 = pl.multiple_of(kb * KB, KB)
        r = jnp.dot(ik_ref[pl.ds(k0, KB), :], iqT_ref[...], preferred_element_type=F32)
        acc = jnp.zeros((KB, QB), F32)
        for h in range(IDX_HEADS):
            acc = acc + iwT_ref[h:h + 1, :] * jnp.maximum(r[:, h * QB:(h + 1) * QB], 0.0)
        acc = acc + 0.0
        bits = pltpu.bitcast(acc, I32)
        key = jnp.where(bits < 0, bits ^ 0x7FFFFFFF, bits)
        admissible = (k0 + key_row) // CHUNK <= q_chunk
        keys_ref[kb] = jnp.where(admissible, key, INT_MIN)
        return carry

    lax.fori_loop(0, 2 * n_trips, score_block, 0)

    def count_ge(cand):
        def body(t, cnt):
            for kb in (2 * t, 2 * t + 1):
                hit = jnp.where(keys_ref[kb] >= cand, 1.0, 0.0)
                cnt = cnt + jnp.sum(hit.reshape(KB // SM_ROWS, SM_ROWS, QB), axis=0)
            return cnt

        cnt = lax.fori_loop(0, n_trips, body, jnp.zeros((SM_ROWS, QB), F32))
        return jnp.sum(cnt, axis=0, keepdims=True).astype(I32)

    zero = jnp.zeros((1, QB), I32)
    t0 = jnp.where(count_ge(zero) >= topk, zero, zero + INT_MIN)

    def bit_step(n, t):
        cand = t + jnp.left_shift(jnp.int32(1), 30 - n)
        return jnp.where(count_ge(cand) >= topk, cand, t)

    thr = lax.fori_loop(0, 31, bit_step, t0)
    n_gt = count_ge(thr + 1)
    need = jnp.where(thr == INT_MIN, 0, topk - n_gt).astype(F32)

    d_row = lax.broadcasted_iota(I32, (LANES, QB), 0)
    eye = jnp.where(d_row == lax.broadcasted_iota(I32, (LANES, QB), 1), 1.0, 0.0).astype(BF16)
    for j in range(n_pairs):
        qT = dqT_ref[j * LANES:(j + 1) * LANES, :]
        rhs_ref[j, :LANES, :QB] = jnp.where(d_row < DSA_DH, qT, jnp.zeros_like(qT))
        rhs_ref[j, :LANES, QB:] = jnp.where(d_row >= DSA_DH, qT, jnp.zeros_like(qT))
        rhs_ref[j, LANES:, :QB] = eye
        rhs_ref[j, LANES:, QB:] = eye
    m_ref[...] = jnp.full(m_ref.shape, M_INIT, F32)
    l_ref[...] = jnp.zeros(l_ref.shape, F32)
    acc_ref[...] = jnp.zeros(acc_ref.shape, F32)

    def logits(blk, slot, n_eq_before):
        kb = jnp.minimum(blk, nkb - 1)
        k0 = pl.multiple_of(kb * KB, KB)
        key = jnp.where(blk < nkb, keys_ref[kb], INT_MIN)
        eq = key == thr
        eq_f = jnp.where(eq, 1.0, 0.0)
        prefix = jnp.dot(tri_ref[...], eq_f.astype(BF16), preferred_element_type=F32)
        keep_tie = (n_eq_before + prefix) < need
        bias = jnp.where(key > thr, 0.0, jnp.where(eq, jnp.where(keep_tie, 0.0, NEG_BIG), NEG_BIG)).astype(BF16)
        for j in range(n_pairs):
            lhs = jnp.concatenate([dk_ref[pl.ds(k0, KB), j * LANES:(j + 1) * LANES], bias], axis=1)
            s_ref[slot, j] = jnp.dot(lhs, rhs_ref[j], preferred_element_type=F32)
        return n_eq_before + jnp.sum(eq_f, axis=0, keepdims=True)

    def softmax_update(slot):
        rows = [slice(r * SM_ROWS, (r + 1) * SM_ROWS) for r in range(KB // SM_ROWS)]
        for j in range(n_pairs):
            mx = s_ref[slot, j, rows[0], :]
            for r in rows[1:]:
                mx = jnp.maximum(mx, s_ref[slot, j, r, :])
            m_prev = m_ref[j]
            m_new = jnp.maximum(m_prev, jnp.max(mx, axis=0, keepdims=True))
            alpha = jnp.exp(m_prev - m_new)
            psum = jnp.zeros((SM_ROWS, 2 * QB), F32)
            for r in rows:
                p = jnp.exp(s_ref[slot, j, r, :] - m_new)
                psum = psum + p
                p_ref[slot, j, r, :] = p.astype(BF16)
            l_ref[j] = alpha * l_ref[j] + jnp.sum(psum, axis=0, keepdims=True)
            m_ref[j] = m_new
            alpha_ref[slot, j] = alpha

    def accumulate(kb, slot):
        for j in range(n_pairs):
            pv = jnp.dot(dvT_ref[kb, j * LANES:(j + 1) * LANES, :], p_ref[slot, j], preferred_element_type=F32)
            acc_ref[j] = alpha_ref[slot, j] * acc_ref[j] + pv

    p_ref[1] = jnp.zeros(p_ref.shape[1:], BF16)
    alpha_ref[1] = jnp.ones(alpha_ref.shape[1:], F32)
    n_eq0 = logits(0, 0, jnp.zeros((1, QB), F32))

    def attend_two_blocks(t, n_eq):
        b = 2 * t
        n_eq = logits(b + 1, 1, n_eq)
        softmax_update(0)
        accumulate(jnp.maximum(b - 1, 0), 1)
        n_eq = logits(b + 2, 0, n_eq)
        softmax_update(1)
        accumulate(b, 0)
        return n_eq

    lax.fori_loop(0, n_trips, attend_two_blocks, n_eq0)
    accumulate(jnp.minimum(2 * n_trips - 1, nkb - 1), 1)

    outs = []
    for j in range(n_pairs):
        acc, l = acc_ref[j], l_ref[j]
        outs.append(acc[:DSA_DH, :QB] / l[:, :QB])
        outs.append(acc[DSA_DH:, QB:] / l[:, QB:])
    y_ref[...] = jnp.concatenate(outs, axis=0).T.astype(y_ref.dtype)


def _dsa(iqT, iwT, ik, dqT, dk, dvT):
    B, L, _ = dk.shape
    topk = min(DSA_TOPK_MAX, L // 4)
    r = jnp.arange(KB)
    tri = (r[None, :] < r[:, None]).astype(BF16)
    per_batch = lambda shape: pl.BlockSpec((None,) + shape, lambda b, q: (b,) + (0,) * len(shape))
    return pl.pallas_call(
        functools.partial(_dsa_kernel, topk=topk),
        grid=(B, L // QB),
        in_specs=[pl.BlockSpec((None, None, IDX_DH, IDX_HEADS * QB), lambda b, q: (b, q, 0, 0)),
                  pl.BlockSpec((None, 8, QB), lambda b, q: (b, 0, q)),
                  per_batch((L, IDX_DH)),
                  pl.BlockSpec((None, DSA_W, QB), lambda b, q: (b, 0, q)),
                  per_batch((L, DSA_W)),
                  per_batch((L // KB, DSA_W, KB)),
                  pl.BlockSpec((KB, KB), lambda b, q: (0, 0))],
        out_specs=pl.BlockSpec((None, QB, DSA_W), lambda b, q: (b, q, 0)),
        out_shape=jax.ShapeDtypeStruct((B, L, DSA_W), BF16),
        scratch_shapes=[pltpu.VMEM((L // KB, KB, QB), I32),
                        pltpu.VMEM((DSA_HEADS // 2, 2 * LANES, 2 * QB), BF16),
                        pltpu.VMEM((DSA_HEADS // 2, 1, 2 * QB), F32),
                        pltpu.VMEM((DSA_HEADS // 2, 1, 2 * QB), F32),
                        pltpu.VMEM((DSA_HEADS // 2, LANES, 2 * QB), F32),
                        pltpu.VMEM((2, DSA_HEADS // 2, KB, 2 * QB), F32),
                        pltpu.VMEM((2, DSA_HEADS // 2, KB, 2 * QB), BF16),
                        pltpu.VMEM((2, DSA_HEADS // 2, 1, 2 * QB), F32)],
        compiler_params=_params(("parallel", "arbitrary")),
        name="dsa",
    )(iqT, iwT, ik, dqT, dk, dvT, tri)


def _merge_kernel(yr_ref, yd_ref, gate_ref, x_ref, wr_ref, wd_ref, wo_ref, fn_ref, x1_ref, h2T_ref, *, d_model):
    g = gate_ref[...]
    br = jnp.dot(yr_ref[...], wr_ref[...], preferred_element_type=F32)
    bd = jnp.dot(yd_ref[...], wd_ref[...], preferred_element_type=F32)
    merged = jax.nn.sigmoid(g[:, :d_model]) * br + jax.nn.sigmoid(g[:, d_model:]) * bd
    x1 = x_ref[...] + jnp.dot(merged.astype(BF16), wo_ref[...], preferred_element_type=F32)
    x1_ref[...] = x1
    ms = jnp.mean(x1 * x1, axis=-1, keepdims=True)
    h2 = x1 * lax.rsqrt(ms + EPS) * fn_ref[...]
    h2T_ref[...] = h2.T.astype(BF16)


def _merge(y_ret, y_dsa, gates, x, w_ret_o, w_dsa_o, w_out, ffn_norm):
    N, D = x.shape
    tm = TM_MERGE
    tok = lambda width: pl.BlockSpec((tm, width), lambda t: (t, 0))
    full = lambda shape: pl.BlockSpec(shape, lambda t: (0, 0))
    return pl.pallas_call(
        functools.partial(_merge_kernel, d_model=D),
        grid=(N // tm,),
        in_specs=[tok(RET_V), tok(DSA_W), tok(2 * D), tok(D),
                  full((RET_V, D)), full((DSA_W, D)), full((D, D)), full((1, D))],
        out_specs=(tok(D), pl.BlockSpec((D, tm), lambda t: (0, t))),
        out_shape=(jax.ShapeDtypeStruct((N, D), F32), jax.ShapeDtypeStruct((D, N), BF16)),
        compiler_params=_params(("parallel",)),
        name="merge",
    )(y_ret, y_dsa, gates, x, w_ret_o.astype(BF16), w_dsa_o.astype(BF16), w_out.astype(BF16),
      ffn_norm.reshape(1, D))


_ROW_LEN = [PEER_TOPK // (a + 1) for a in range(PEER_TOPK)]


def _extract_top(s, n):
    rows = lax.broadcasted_iota(I32, s.shape, 0)
    rank = jnp.full(s.shape, float(n), F32)
    vals = []
    for a in range(n):
        m = jnp.max(s, axis=0, keepdims=True)
        first = jnp.min(jnp.where(s == m, rows, s.shape[0]), axis=0, keepdims=True)
        hit = rows == first
        s = jnp.where(hit, -jnp.inf, s)
        rank = jnp.where(hit, float(a), rank)
        vals.append(m)
    return vals, rank


def _peer_route_kernel(h2T_ref, wqT_ref, sk_ref, r1_ref, p1_ref, n_ref, p0_ref):
    h2T = h2T_ref[...]
    K = PEER_TOPK
    for h in range(PEER_HEADS):
        sT = []
        for c in range(2):
            r0 = (h * 2 + c) * PEER_NKEYS
            qT = jnp.dot(wqT_ref[r0:r0 + PEER_NKEYS, :], h2T, preferred_element_type=F32)
            sT.append(jnp.dot(sk_ref[h, c], qT.astype(BF16), preferred_element_type=F32))
        v0, rank0 = _extract_top(sT[0], K)
        v1, rank1 = _extract_top(sT[1], K)
        cand = [[v0[a] + v1[b] for b in range(_ROW_LEN[a])] for a in range(K)]
        best, _ = _extract_top(jnp.concatenate([c for row in cand for c in row], axis=0), K)
        tau = best[K - 1]
        z = jnp.zeros_like(tau)
        for k in range(K):
            z = z + jnp.exp(best[k] - best[0])
        gt = [sum(jnp.where(c > tau, 1.0, 0.0) for c in row) for row in cand]
        eq = [sum(jnp.where(c == tau, 1.0, 0.0) for c in row) for row in cand]
        need = float(K) - sum(gt)
        n = jnp.zeros(rank0.shape, F32)
        for a in range(K):
            n_a = gt[a] + jnp.minimum(jnp.maximum(need, 0.0), eq[a])
            need = need - eq[a]
            n = jnp.where(rank0 == float(a), n_a, n)
        r1_ref[h] = rank1.astype(BF16)
        p1_ref[h] = jnp.exp(sT[1] - v1[0]).astype(BF16)
        n_ref[h] = n.astype(BF16)
        p0_ref[h] = (jnp.exp(sT[0] - v0[0]) / z).astype(BF16)


def _peer_route(h2T, peer_wq, peer_subkeys):
    D, N = h2T.shape
    tm = TM_ROUTE
    wqT = peer_wq.T.astype(BF16)
    sk = peer_subkeys.astype(BF16)
    spec = pl.BlockSpec((PEER_HEADS, PEER_NKEYS, tm), lambda t: (0, 0, t))
    shape = lambda dt: jax.ShapeDtypeStruct((PEER_HEADS, PEER_NKEYS, N), dt)
    return pl.pallas_call(
        _peer_route_kernel,
        grid=(N // tm,),
        in_specs=[pl.BlockSpec((D, tm), lambda t: (0, t)),
                  pl.BlockSpec(wqT.shape, lambda t: (0, 0)),
                  pl.BlockSpec(sk.shape, lambda t: (0, 0, 0, 0))],
        out_specs=(spec, spec, spec, spec),
        out_shape=(shape(BF16), shape(BF16), shape(BF16), shape(BF16)),
        compiler_params=_params(("parallel",)),
        name="peer_route",
    )(h2T, wqT, sk)


def _peer_dense_kernel(h2T_ref, r1_ref, p1_ref, n_ref, p0_ref, u_ref, vT_ref, x1_ref, fn_ref, out_ref,
                       acc_ref, a_ref, g_ref, r1s_ref, p1s_ref, *, n_groups):
    step = pl.program_id(1)
    tm = h2T_ref.shape[1]

    def project(slot):
        a_ref[slot] = jnp.dot(u_ref[...], h2T_ref[...], preferred_element_type=F32)

    def gate(slot):
        for c in range(tm // LANES):
            sl = slice(c * LANES, (c + 1) * LANES)
            for il in range(IG_PEER):
                rows = slice(il * PEER_NKEYS, (il + 1) * PEER_NKEYS)
                w = jnp.zeros((PEER_NKEYS, LANES), BF16)
                for h in range(PEER_HEADS):
                    n_i = n_ref[h, il:il + 1, sl]
                    p0 = p0_ref[h, il:il + 1, sl]
                    w = w + jnp.where(r1s_ref[h, :, sl] < n_i, p1s_ref[h, :, sl] * p0, jnp.zeros_like(w))
                a = a_ref[slot, rows, sl]
                gelu = 0.5 * a * (1.0 + lax.erf(a * (2.0 ** -0.5)))
                g_ref[slot, rows, sl] = (w.astype(F32) * gelu).astype(BF16)

    def combine(slot):
        acc_ref[...] += jnp.dot(vT_ref[...], g_ref[slot], preferred_element_type=F32)

    @pl.when(step == 0)
    def _():
        acc_ref[...] = jnp.zeros_like(acc_ref)
        g_ref[1] = jnp.zeros(g_ref.shape[1:], BF16)
        r1s_ref[...] = r1_ref[...]
        p1s_ref[...] = p1_ref[...]
        project(0)

    @pl.when(jnp.logical_and(step > 0, step <= n_groups))
    def _():
        project(step % 2)
        gate((step - 1) % 2)
        combine(step % 2)

    @pl.when(step == n_groups + 1)
    def _():
        combine(step % 2)
        y = x1_ref[...] + acc_ref[...].T
        ms = jnp.mean(y * y, axis=-1, keepdims=True)
        out_ref[...] = y * lax.rsqrt(ms + EPS) * fn_ref[...]


def _peer_dense(h2T, r1, p1, n, p0, peer_u, peer_v, x1, final_norm):
    D, N = h2T.shape
    tm = TM_PEER
    n_exp = peer_u.shape[0]
    u = peer_u.astype(BF16)
    vT = peer_v.T.astype(BF16)
    rows = IG_PEER * PEER_NKEYS
    ng = n_exp // rows
    rt = pl.BlockSpec((PEER_HEADS, PEER_NKEYS, tm), lambda t, s: (0, 0, t))
    grp = pl.BlockSpec((PEER_HEADS, None, IG_PEER, tm), lambda t, s: (0, jnp.clip(s - 1, 0, ng - 1), 0, t))
    n = n.reshape(PEER_HEADS, ng, IG_PEER, N)
    p0 = p0.reshape(PEER_HEADS, ng, IG_PEER, N)
    return pl.pallas_call(
        functools.partial(_peer_dense_kernel, n_groups=ng),
        grid=(N // tm, ng + 2),
        in_specs=[pl.BlockSpec((D, tm), lambda t, s: (0, t)), rt, rt, grp, grp,
                  pl.BlockSpec((rows, D), lambda t, s: (jnp.minimum(s, ng - 1), 0)),
                  pl.BlockSpec((D, rows), lambda t, s: (0, jnp.clip(s - 2, 0, ng - 1))),
                  pl.BlockSpec((tm, D), lambda t, s: (t, 0)),
                  pl.BlockSpec((1, D), lambda t, s: (0, 0))],
        out_specs=pl.BlockSpec((tm, D), lambda t, s: (t, 0)),
        out_shape=jax.ShapeDtypeStruct((N, D), F32),
        scratch_shapes=[pltpu.VMEM((D, tm), F32), pltpu.VMEM((2, rows, tm), F32), pltpu.VMEM((2, rows, tm), BF16),
                        pltpu.VMEM((PEER_HEADS, PEER_NKEYS, tm), BF16), pltpu.VMEM((PEER_HEADS, PEER_NKEYS, tm), BF16)],
        compiler_params=_params(("parallel", "arbitrary")),
        name="peer_dense",
    )(h2T, r1, p1, n, p0, u, vT, x1, final_norm.reshape(1, D))


def kernel(x, attn_norm, w_in, ret_gn, w_ret_o, w_dsa_o, w_out, ffn_norm, peer_wq, peer_subkeys, peer_u, peer_v,
           final_norm):
    B, L, D = x.shape
    assert attn_norm.shape[0] == 1, "single-layer block"
    assert L % TM_IN == 0 and (B * L) % TM_PEER == 0 and D % LANES == 0 and QB == LANES
    rq, rk, rv, rg, dqT, dk, dvT, iqT, ik, iwT, gates = _inproj(x, attn_norm[0], w_in[0])
    y_ret = _retention(rq, rk, rv, rg, ret_gn[0])
    y_dsa = _dsa(iqT, iwT, ik, dqT, dk, dvT)
    x1, h2T = _merge(y_ret.reshape(B * L, RET_V), y_dsa.reshape(B * L, DSA_W), gates.reshape(B * L, 2 * D),
                     x.reshape(B * L, D), w_ret_o[0], w_dsa_o[0], w_out[0], ffn_norm[0])
    r1, p1, n, p0 = _peer_route(h2T, peer_wq[0], peer_subkeys[0])
    out = _peer_dense(h2T, r1, p1, n, p0, peer_u[0], peer_v[0], x1, final_norm)
    return out.reshape(B, L, D)
```

```python
import functools

import jax
import jax.numpy as jnp
import numpy as np
from jax import lax
from jax.experimental import pallas as pl
from jax.experimental.pallas import tpu as pltpu

F32 = jnp.float32
BF16 = jnp.bfloat16
I32 = jnp.int32

CHUNK = 64
EPS = 1e-6
RET_HEADS, RET_DK, RET_DV, RET_THETA = 4, 64, 128, 10000.0
DSA_HEADS, DSA_DH, IDX_HEADS, IDX_DH = 8, 64, 4, 64
DSA_TOPK_MAX = 256
ROPE_THETA, ROPE_DIM = 500000.0, 16
PEER_HEADS, PEER_NKEYS, PEER_DKEY, PEER_TOPK = 8, 128, 256, 16

LANES = 128
VMEM_LIMIT = 56 * 1024 * 1024

RET_QK = RET_HEADS * RET_DK
RET_V = RET_HEADS * RET_DV
DSA_W = DSA_HEADS * DSA_DH
IDX_Q = IDX_HEADS * IDX_DH

TM_IN = 512
TB_RET = 256
QB = 128
KB = 256
SM_ROWS = 64
TM_MERGE = 512
TM_ROUTE = 256
TM_PEER = 512
IG_PEER = 8
GATE_KEYS = 2
MXU_PIECES = 4

NEG_BIG = -1e30
M_INIT = -1e20
INT_MIN = -(2 ** 31)

_OFF = {}
_o = 0
for _name, _w in (("rq", RET_QK), ("rk", RET_QK), ("rv", RET_V), ("rg", RET_V), ("dq", DSA_W), ("dk", DSA_W),
                  ("dv", DSA_W), ("iq", IDX_Q), ("ik", LANES), ("iw", LANES), ("gates", None)):
    _OFF[_name] = _o
    if _w is not None:
        _o += _w


def _params(sem):
    return pltpu.CompilerParams(dimension_semantics=sem, vmem_limit_bytes=VMEM_LIMIT)


def _rope_chunk(p, cos, sin, half):
    lane = lax.broadcasted_iota(I32, p.shape, 1) % 64
    partner = jnp.where(lane < half, pltpu.roll(p, LANES - half, 1), pltpu.roll(p, half, 1))
    return p * cos + partner * sin


def _inproj_kernel(x_ref, g_ref, w_ref, cr_ref, sr_ref, cp_ref, sp_ref,
                   rq_ref, rk_ref, rv_ref, rg_ref, dqT_ref, dk_ref, dvT_ref, iqT_ref, ik_ref, iwT_ref, gate_ref,
                   *, d_model):
    x = x_ref[...]
    ms = jnp.mean(x * x, axis=-1, keepdims=True)
    h = (x * lax.rsqrt(ms + EPS) * g_ref[...]).astype(BF16)

    def proj(off, width):
        return jnp.dot(h, w_ref[:, off:off + width], preferred_element_type=F32)

    cr, sr, cp, sp = cr_ref[...], sr_ref[...], cp_ref[...], sp_ref[...]
    qk_scale = RET_DK ** -0.5
    for c in range(RET_QK // LANES):
        sl = slice(c * LANES, (c + 1) * LANES)
        rq_ref[:, sl] = _rope_chunk(proj(_OFF["rq"] + c * LANES, LANES), cr, sr, RET_DK // 2) * qk_scale
        rk_ref[:, sl] = _rope_chunk(proj(_OFF["rk"] + c * LANES, LANES), cr, sr, RET_DK // 2)
    rv_ref[...] = proj(_OFF["rv"], RET_V).astype(BF16)
    rg_ref[...] = proj(_OFF["rg"], RET_V)
    tm = x.shape[0]
    dh_scale = DSA_DH ** -0.5
    for c in range(DSA_W // LANES):
        sl = slice(c * LANES, (c + 1) * LANES)
        dq = _rope_chunk(proj(_OFF["dq"] + c * LANES, LANES), cp, sp, ROPE_DIM // 2) * dh_scale
        dqT_ref[sl, :] = dq.T.astype(BF16)
        dk_ref[:, sl] = _rope_chunk(proj(_OFF["dk"] + c * LANES, LANES), cp, sp, ROPE_DIM // 2).astype(BF16)
        dv = proj(_OFF["dv"] + c * LANES, LANES)
        for j in range(tm // KB):
            dvT_ref[j, sl, :] = dv[j * KB:(j + 1) * KB, :].T.astype(BF16)
    for c in range(IDX_Q // LANES):
        iq = _rope_chunk(proj(_OFF["iq"] + c * LANES, LANES), cp, sp, ROPE_DIM // 2)
        for j in range(tm // QB):
            t = iq[j * QB:(j + 1) * QB, :].T.astype(BF16)
            iqT_ref[j, :, (2 * c) * QB:(2 * c + 1) * QB] = t[:IDX_DH, :]
            iqT_ref[j, :, (2 * c + 1) * QB:(2 * c + 2) * QB] = t[IDX_DH:, :]
    ik = _rope_chunk(proj(_OFF["ik"], LANES), cp, sp, ROPE_DIM // 2)
    ik_ref[...] = ik[:, :IDX_DH].astype(BF16)
    iw = proj(_OFF["iw"], LANES) * ((IDX_HEADS ** -0.5) * (IDX_DH ** -0.5))
    iwT_ref[...] = iw.T[:8, :]
    gate_ref[...] = proj(_OFF["gates"], 2 * d_model)


def _rope_tables(L, half, theta, pass_dims):
    inv = theta ** (-jnp.arange(half, dtype=F32) / half)
    ang = jnp.arange(L).astype(F32)[:, None] * inv[None, :]
    cos, sin = jnp.cos(ang), jnp.sin(ang)
    ones = jnp.ones((L, pass_dims), F32)
    zeros = jnp.zeros((L, pass_dims), F32)
    cos64 = jnp.concatenate([cos, cos, ones], axis=-1)
    sin64 = jnp.concatenate([-sin, sin, zeros], axis=-1)
    return jnp.tile(cos64, (1, 2)), jnp.tile(sin64, (1, 2))


def _inproj(x, attn_norm, w_in):
    B, L, D = x.shape
    tm = TM_IN
    nl = L // tm
    sizes = (RET_QK, RET_QK, RET_V, RET_V, DSA_W, DSA_W, DSA_W, IDX_Q, IDX_DH, IDX_HEADS, 2 * D)
    offs = np.concatenate([[0], np.cumsum(sizes)])
    cols = [w_in[:, offs[i]:offs[i + 1]] for i in range(len(sizes))]
    cols[8] = jnp.pad(cols[8], ((0, 0), (0, LANES - IDX_DH)))
    cols[9] = jnp.pad(cols[9], ((0, 0), (0, LANES - IDX_HEADS)))
    w = jnp.concatenate(cols, axis=1).astype(BF16)
    wtot = w.shape[1]
    cr, sr = _rope_tables(L, RET_DK // 2, RET_THETA, 0)
    cp, sp = _rope_tables(L, ROPE_DIM // 2, ROPE_THETA, DSA_DH - ROPE_DIM)

    tok = lambda width: pl.BlockSpec((None, tm, width), lambda b, l: (b, l, 0))
    tab = pl.BlockSpec((tm, LANES), lambda b, l: (l, 0))
    out_shape = (
        jax.ShapeDtypeStruct((B, L, RET_QK), F32),
        jax.ShapeDtypeStruct((B, L, RET_QK), F32),
        jax.ShapeDtypeStruct((B, L, RET_V), BF16),
        jax.ShapeDtypeStruct((B, L, RET_V), F32),
        jax.ShapeDtypeStruct((B, DSA_W, L), BF16),
        jax.ShapeDtypeStruct((B, L, DSA_W), BF16),
        jax.ShapeDtypeStruct((B, L // KB, DSA_W, KB), BF16),
        jax.ShapeDtypeStruct((B, L // QB, IDX_DH, IDX_HEADS * QB), BF16),
        jax.ShapeDtypeStruct((B, L, IDX_DH), BF16),
        jax.ShapeDtypeStruct((B, 8, L), F32),
        jax.ShapeDtypeStruct((B, L, 2 * D), F32),
    )
    out_specs = (
        tok(RET_QK), tok(RET_QK), tok(RET_V), tok(RET_V),
        pl.BlockSpec((None, DSA_W, tm), lambda b, l: (b, 0, l)),
        tok(DSA_W),
        pl.BlockSpec((None, tm // KB, DSA_W, KB), lambda b, l: (b, l, 0, 0)),
        pl.BlockSpec((None, tm // QB, IDX_DH, IDX_HEADS * QB), lambda b, l: (b, l, 0, 0)),
        tok(IDX_DH),
        pl.BlockSpec((None, 8, tm), lambda b, l: (b, 0, l)),
        tok(2 * D),
    )
    return pl.pallas_call(
        functools.partial(_inproj_kernel, d_model=D),
        grid=(B, nl),
        in_specs=[tok(D), pl.BlockSpec((1, D), lambda b, l: (0, 0)),
                  pl.BlockSpec((D, wtot), lambda b, l: (0, 0)), tab, tab, tab, tab],
        out_specs=out_specs,
        out_shape=out_shape,
        compiler_params=_params(("parallel", "parallel")),
        name="inproj",
    )(x, attn_norm.reshape(1, D), w, cr, sr, cp, sp)


def _retention_kernel(q_ref, k_ref, v_ref, g_ref, gn_ref, dm_ref, qd_ref, kd_ref, cd_ref, y_ref, state_ref):
    @pl.when(pl.program_id(1) == 0)
    def _():
        state_ref[...] = jnp.zeros_like(state_ref)

    for h in range(RET_HEADS):
        q = q_ref[:, h * RET_DK:(h + 1) * RET_DK]
        k = k_ref[:, h * RET_DK:(h + 1) * RET_DK]
        v = v_ref[:, h * RET_DV:(h + 1) * RET_DV]
        s = lax.dot_general(q.astype(BF16), k.astype(BF16), (((1,), (1,)), ((), ())), preferred_element_type=F32)
        intra = jnp.dot((s * dm_ref[h]).astype(BF16), v, preferred_element_type=F32)
        state = state_ref[h]
        cross = jnp.dot((q * qd_ref[h]).astype(BF16), state.astype(BF16), preferred_element_type=F32)
        kv = lax.dot_general((k * kd_ref[h]).astype(BF16), v, (((0,), (0,)), ((), ())), preferred_element_type=F32)
        state_ref[h] = cd_ref[h] * state + kv
        y = intra + cross
        mu = jnp.mean(y, axis=-1, keepdims=True)
        yc = y - mu
        var = jnp.mean(yc * yc, axis=-1, keepdims=True)
        sl = slice(h * RET_DV, (h + 1) * RET_DV)
        yn = yc * lax.rsqrt(var + EPS) * gn_ref[:, sl]
        g = g_ref[:, sl]
        y_ref[:, sl] = (g * jax.nn.sigmoid(g) * yn).astype(y_ref.dtype)


def _retention(rq, rk, rv, rg, ret_gn):
    B, L, _ = rq.shape
    tb = TB_RET
    log_g = jnp.log(1.0 - 2.0 ** (-5.0 - jnp.arange(RET_HEADS, dtype=F32)))
    i = jnp.arange(tb, dtype=F32)
    ci = jnp.arange(tb) // CHUNK
    visible = (ci[None, :] <= ci[:, None]).astype(F32)
    dm = jnp.exp(log_g[:, None, None] * jnp.abs(i[:, None] - i[None, :])) * visible[None]
    ones = jnp.ones((1, 1, RET_DK), F32)
    qd = jnp.exp(log_g[:, None] * (i + 1.0)[None, :])[:, :, None] * ones
    kd = jnp.exp(log_g[:, None] * (tb - 1.0 - i)[None, :])[:, :, None] * ones
    cd = jnp.exp(log_g * tb)[:, None, None] * jnp.ones((1, RET_DK, RET_DV), F32)

    tok = lambda width: pl.BlockSpec((None, tb, width), lambda b, l: (b, l, 0))
    full = lambda shape: pl.BlockSpec(shape, lambda b, l: (0,) * len(shape))
    return pl.pallas_call(
        _retention_kernel,
        grid=(B, L // tb),
        in_specs=[tok(RET_QK), tok(RET_QK), tok(RET_V), tok(RET_V), full((1, RET_V)),
                  full((RET_HEADS, tb, tb)), full((RET_HEADS, tb, RET_DK)), full((RET_HEADS, tb, RET_DK)),
                  full((RET_HEADS, RET_DK, RET_DV))],
        out_specs=tok(RET_V),
        out_shape=jax.ShapeDtypeStruct((B, L, RET_V), BF16),
        scratch_shapes=[pltpu.VMEM((RET_HEADS, RET_DK, RET_DV), F32)],
        compiler_params=_params(("parallel", "arbitrary")),
        name="retention",
    )(rq, rk, rv, rg, ret_gn.reshape(1, RET_V), dm, qd, kd, cd)


def _dsa_kernel(iqT_ref, iwT_ref, ik_ref, dqT_ref, dk_ref, dvT_ref, tri_ref, y_ref,
                keys_ref, rhs_ref, m_ref, l_ref, acc_ref, s_ref, p_ref, alpha_ref, *, topk):
    qb = pl.program_id(1)
    nkb = (qb * QB) // KB + 1
    n_trips = (nkb + 1) // 2
    n_pairs = DSA_HEADS // 2

    key_row = lax.broadcasted_iota(I32, (KB, QB), 0)
    q_chunk = (qb * QB + lax.broadcasted_iota(I32, (KB, QB), 1)) // CHUNK

    def score_block(kb):
        k0 = pl.multiple_of(kb * KB, KB)
        r = jnp.dot(ik_ref[pl.ds(k0, KB), :], iqT_ref[...], preferred_element_type=F32)
        acc = jnp.zeros((KB, QB), F32)
        for h in range(IDX_HEADS):
            acc = acc + iwT_ref[h:h + 1, :] * jnp.maximum(r[:, h * QB:(h + 1) * QB], 0.0)
        acc = acc + 0.0
        bits = pltpu.bitcast(acc, I32)
        key = jnp.where(bits < 0, bits ^ 0x7FFFFFFF, bits)
        admissible = (k0 + key_row) // CHUNK <= q_chunk
        keys_ref[kb] = jnp.where(admissible, key, INT_MIN)

    def score_two_blocks(t, carry):
        score_block(2 * t)
        score_block(2 * t + 1)
        return carry

    lax.fori_loop(0, n_trips, score_two_blocks, 0)

    def count_ge(cand):
        def body(t, cnt):
            for kb in (2 * t, 2 * t + 1):
                hit = jnp.where(keys_ref[kb] >= cand, 1.0, 0.0)
                cnt = cnt + jnp.sum(hit.reshape(KB // SM_ROWS, SM_ROWS, QB), axis=0)
            return cnt

        cnt = lax.fori_loop(0, n_trips, body, jnp.zeros((SM_ROWS, QB), F32))
        return jnp.sum(cnt, axis=0, keepdims=True).astype(I32)

    zero = jnp.zeros((1, QB), I32)
    t0 = jnp.where(count_ge(zero) >= topk, zero, zero + INT_MIN)

    def bit_step(n, t):
        cand = t + jnp.left_shift(jnp.int32(1), 30 - n)
        return jnp.where(count_ge(cand) >= topk, cand, t)

    thr = lax.fori_loop(0, 31, bit_step, t0)
    n_gt = count_ge(thr + 1)
    need = jnp.where(thr == INT_MIN, 0, topk - n_gt).astype(F32)

    d_row = lax.broadcasted_iota(I32, (LANES, QB), 0)
    eye = jnp.where(d_row == lax.broadcasted_iota(I32, (LANES, QB), 1), 1.0, 0.0).astype(BF16)
    for j in range(n_pairs):
        qT = dqT_ref[j * LANES:(j + 1) * LANES, :]
        rhs_ref[j, :LANES, :QB] = jnp.where(d_row < DSA_DH, qT, jnp.zeros_like(qT))
        rhs_ref[j, :LANES, QB:] = jnp.where(d_row >= DSA_DH, qT, jnp.zeros_like(qT))
        rhs_ref[j, LANES:, :QB] = eye
        rhs_ref[j, LANES:, QB:] = eye
    m_ref[...] = jnp.full(m_ref.shape, M_INIT, F32)
    l_ref[...] = jnp.zeros(l_ref.shape, F32)
    acc_ref[...] = jnp.zeros(acc_ref.shape, F32)

    def logits(blk, slot, n_eq_before):
        kb = jnp.minimum(blk, nkb - 1)
        k0 = pl.multiple_of(kb * KB, KB)
        key = jnp.where(blk < nkb, keys_ref[kb], INT_MIN)
        eq = key == thr
        eq_f = jnp.where(eq, 1.0, 0.0)
        prefix = jnp.dot(tri_ref[...], eq_f.astype(BF16), preferred_element_type=F32)
        keep_tie = (n_eq_before + prefix) < need
        bias = jnp.where(key > thr, 0.0, jnp.where(eq, jnp.where(keep_tie, 0.0, NEG_BIG), NEG_BIG)).astype(BF16)
        for j in range(n_pairs):
            lhs = jnp.concatenate([dk_ref[pl.ds(k0, KB), j * LANES:(j + 1) * LANES], bias], axis=1)
            s_ref[slot, j] = jnp.dot(lhs, rhs_ref[j], preferred_element_type=F32)
        return n_eq_before + jnp.sum(eq_f, axis=0, keepdims=True)

    def softmax_update(slot):
        rows = [slice(r * SM_ROWS, (r + 1) * SM_ROWS) for r in range(KB // SM_ROWS)]
        for j in range(n_pairs):
            mx = s_ref[slot, j, rows[0], :]
            for r in rows[1:]:
                mx = jnp.maximum(mx, s_ref[slot, j, r, :])
            m_prev = m_ref[j]
            m_new = jnp.maximum(m_prev, jnp.max(mx, axis=0, keepdims=True))
            alpha = jnp.exp(m_prev - m_new)
            psum = jnp.zeros((SM_ROWS, 2 * QB), F32)
            for r in rows:
                p = jnp.exp(s_ref[slot, j, r, :] - m_new)
                psum = psum + p
                p_ref[slot, j, r, :] = p.astype(BF16)
            l_ref[j] = alpha * l_ref[j] + jnp.sum(psum, axis=0, keepdims=True)
            m_ref[j] = m_new
            alpha_ref[slot, j] = alpha

    def accumulate(kb, slot):
        for j in range(n_pairs):
            pv = jnp.dot(dvT_ref[kb, j * LANES:(j + 1) * LANES, :], p_ref[slot, j], preferred_element_type=F32)
            acc_ref[j] = alpha_ref[slot, j] * acc_ref[j] + pv

    p_ref[1] = jnp.zeros(p_ref.shape[1:], BF16)
    alpha_ref[1] = jnp.ones(alpha_ref.shape[1:], F32)
    n_eq0 = logits(0, 0, jnp.zeros((1, QB), F32))

    def attend_two_blocks(t, n_eq):
        b = 2 * t
        n_eq = logits(b + 1, 1, n_eq)
        softmax_update(0)
        accumulate(jnp.maximum(b - 1, 0), 1)
        n_eq = logits(b + 2, 0, n_eq)
        softmax_update(1)
        accumulate(b, 0)
        return n_eq

    lax.fori_loop(0, n_trips, attend_two_blocks, n_eq0)
    accumulate(jnp.minimum(2 * n_trips - 1, nkb - 1), 1)

    outs = []
    for j in range(n_pairs):
        acc, l = acc_ref[j], l_ref[j]
        outs.append(acc[:DSA_DH, :QB] / l[:, :QB])
        outs.append(acc[DSA_DH:, QB:] / l[:, QB:])
    y_ref[...] = jnp.concatenate(outs, axis=0).T.astype(y_ref.dtype)


def _dsa(iqT, iwT, ik, dqT, dk, dvT):
    B, L, _ = dk.shape
    topk = min(DSA_TOPK_MAX, L // 4)
    r = jnp.arange(KB)
    tri = (r[None, :] < r[:, None]).astype(BF16)
    per_batch = lambda shape: pl.BlockSpec((None,) + shape, lambda b, q: (b,) + (0,) * len(shape))
    return pl.pallas_call(
        functools.partial(_dsa_kernel, topk=topk),
        grid=(B, L // QB),
        in_specs=[pl.BlockSpec((None, None, IDX_DH, IDX_HEADS * QB), lambda b, q: (b, q, 0, 0)),
                  pl.BlockSpec((None, 8, QB), lambda b, q: (b, 0, q)),
                  per_batch((L, IDX_DH)),
                  pl.BlockSpec((None, DSA_W, QB), lambda b, q: (b, 0, q)),
                  per_batch((L, DSA_W)),
                  per_batch((L // KB, DSA_W, KB)),
                  pl.BlockSpec((KB, KB), lambda b, q: (0, 0))],
        out_specs=pl.BlockSpec((None, QB, DSA_W), lambda b, q: (b, q, 0)),
        out_shape=jax.ShapeDtypeStruct((B, L, DSA_W), BF16),
        scratch_shapes=[pltpu.VMEM((L // KB, KB, QB), I32),
                        pltpu.VMEM((DSA_HEADS // 2, 2 * LANES, 2 * QB), BF16),
                        pltpu.VMEM((DSA_HEADS // 2, 1, 2 * QB), F32),
                        pltpu.VMEM((DSA_HEADS // 2, 1, 2 * QB), F32),
                        pltpu.VMEM((DSA_HEADS // 2, LANES, 2 * QB), F32),
                        pltpu.VMEM((2, DSA_HEADS // 2, KB, 2 * QB), F32),
                        pltpu.VMEM((2, DSA_HEADS // 2, KB, 2 * QB), BF16),
                        pltpu.VMEM((2, DSA_HEADS // 2, 1, 2 * QB), F32)],
        compiler_params=_params(("parallel", "arbitrary")),
        name="dsa",
    )(iqT, iwT, ik, dqT, dk, dvT, tri)


def _merge_kernel(yr_ref, yd_ref, gate_ref, x_ref, wr_ref, wd_ref, wo_ref, fn_ref, x1_ref, h2T_ref, *, d_model):
    g = gate_ref[...]
    br = jnp.dot(yr_ref[...], wr_ref[...], preferred_element_type=F32)
    bd = jnp.dot(yd_ref[...], wd_ref[...], preferred_element_type=F32)
    merged = jax.nn.sigmoid(g[:, :d_model]) * br + jax.nn.sigmoid(g[:, d_model:]) * bd
    x1 = x_ref[...] + jnp.dot(merged.astype(BF16), wo_ref[...], preferred_element_type=F32)
    x1_ref[...] = x1
    ms = jnp.mean(x1 * x1, axis=-1, keepdims=True)
    h2 = x1 * lax.rsqrt(ms + EPS) * fn_ref[...]
    h2T_ref[...] = h2.T.astype(BF16)


def _merge(y_ret, y_dsa, gates, x, w_ret_o, w_dsa_o, w_out, ffn_norm):
    N, D = x.shape
    tm = TM_MERGE
    tok = lambda width: pl.BlockSpec((tm, width), lambda t: (t, 0))
    full = lambda shape: pl.BlockSpec(shape, lambda t: (0, 0))
    return pl.pallas_call(
        functools.partial(_merge_kernel, d_model=D),
        grid=(N // tm,),
        in_specs=[tok(RET_V), tok(DSA_W), tok(2 * D), tok(D),
                  full((RET_V, D)), full((DSA_W, D)), full((D, D)), full((1, D))],
        out_specs=(tok(D), pl.BlockSpec((D, tm), lambda t: (0, t))),
        out_shape=(jax.ShapeDtypeStruct((N, D), F32), jax.ShapeDtypeStruct((D, N), BF16)),
        compiler_params=_params(("parallel",)),
        name="merge",
    )(y_ret, y_dsa, gates, x, w_ret_o.astype(BF16), w_dsa_o.astype(BF16), w_out.astype(BF16),
      ffn_norm.reshape(1, D))


_ROW_LEN = [PEER_TOPK // (a + 1) for a in range(PEER_TOPK)]


def _extract_rounds(s, n, break_ties):
    rows = lax.broadcasted_iota(I32, s.shape, 0).astype(F32)
    rank = jnp.full(s.shape, float(n), F32)
    vals = []
    for a in range(n):
        m = jnp.max(s, axis=0, keepdims=True)
        hit = s == m
        if break_ties:
            first = jnp.min(jnp.where(hit, rows, float(s.shape[0])), axis=0, keepdims=True)
            hit = rows == first
        s = jnp.where(hit, -jnp.inf, s)
        rank = jnp.where(hit, float(a), rank)
        vals.append(m)
    return vals, rank


def _extract_top_pair(s_a, s_b, n):
    fast = (_extract_rounds(s_a, n, break_ties=False), _extract_rounds(s_b, n, break_ties=False))
    taken = sum(jnp.sum(jnp.where(rank < float(n), 1.0, 0.0), axis=0, keepdims=True) for _, rank in fast)
    return lax.cond(jnp.max(taken) > float(2 * n),
                    lambda: (_extract_rounds(s_a, n, break_ties=True), _extract_rounds(s_b, n, break_ties=True)),
                    lambda: fast)


def _peer_route_kernel(h2T_ref, wqT_ref, sk_ref, r1_ref, p1_ref, n_ref, p0_ref):
    h2T = h2T_ref[...]
    K = PEER_TOPK
    for h in range(PEER_HEADS):
        sT = []
        for c in range(2):
            r0 = (h * 2 + c) * PEER_NKEYS
            qT = jnp.dot(wqT_ref[r0:r0 + PEER_NKEYS, :], h2T, preferred_element_type=F32)
            sT.append(jnp.dot(sk_ref[h, c], qT.astype(BF16), preferred_element_type=F32))
        (v0, rank0), (v1, rank1) = _extract_top_pair(sT[0], sT[1], K)
        cand = [[v0[a] + v1[b] for b in range(_ROW_LEN[a])] for a in range(K)]
        best, _ = _extract_rounds(jnp.concatenate([c for row in cand for c in row], axis=0), K, break_ties=True)
        tau = best[K - 1]
        z = jnp.zeros_like(tau)
        for k in range(K):
            z = z + jnp.exp(best[k] - best[0])
        gt = [sum(jnp.where(c > tau, 1.0, 0.0) for c in row) for row in cand]
        eq = [sum(jnp.where(c == tau, 1.0, 0.0) for c in row) for row in cand]
        need = float(K) - sum(gt)
        n = jnp.zeros(rank0.shape, F32)
        for a in range(K):
            n_a = gt[a] + jnp.minimum(jnp.maximum(need, 0.0), eq[a])
            need = need - eq[a]
            n = jnp.where(rank0 == float(a), n_a, n)
        r1_ref[h] = rank1.astype(BF16)
        p1_ref[h] = jnp.exp(sT[1] - v1[0]).astype(BF16)
        n_ref[h] = n.astype(BF16)
        p0_ref[h] = (jnp.exp(sT[0] - v0[0]) / z).astype(BF16)


def _peer_route(h2T, peer_wq, peer_subkeys):
    D, N = h2T.shape
    tm = TM_ROUTE
    wqT = peer_wq.T.astype(BF16)
    sk = peer_subkeys.astype(BF16)
    spec = pl.BlockSpec((PEER_HEADS, PEER_NKEYS, tm), lambda t: (0, 0, t))
    shape = lambda dt: jax.ShapeDtypeStruct((PEER_HEADS, PEER_NKEYS, N), dt)
    return pl.pallas_call(
        _peer_route_kernel,
        grid=(N // tm,),
        in_specs=[pl.BlockSpec((D, tm), lambda t: (0, t)),
                  pl.BlockSpec(wqT.shape, lambda t: (0, 0)),
                  pl.BlockSpec(sk.shape, lambda t: (0, 0, 0, 0))],
        out_specs=(spec, spec, spec, spec),
        out_shape=(shape(BF16), shape(BF16), shape(BF16), shape(BF16)),
        compiler_params=_params(("parallel",)),
        name="peer_route",
    )(h2T, wqT, sk)


def _peer_dense_kernel(h2T_ref, r1_ref, p1_ref, n_a_ref, n_b_ref, p0_a_ref, p0_b_ref, u_a_ref, u_b_ref,
                       vT_a_ref, vT_b_ref, x1_ref, fn_ref, out_ref, acc_ref, a_ref, g_ref, r1s_ref, p1s_ref,
                       *, n_trips):
    step = pl.program_id(1)
    tm = h2T_ref.shape[1]

    d_model = acc_ref.shape[0]
    group_rows = IG_PEER * PEER_NKEYS
    gate_blocks = [(c, i0) for c in range(tm // LANES) for i0 in range(0, IG_PEER, GATE_KEYS)]

    def project(u_ref, slot, piece, pieces):
        rows = pl.ds(piece * (group_rows // pieces), group_rows // pieces)
        a_ref[slot, rows, :] = jnp.dot(u_ref[rows, :], h2T_ref[...], preferred_element_type=F32)

    def gate_block(n_ref, p0_ref, row0, slot, c, i0):
        sl = slice(c * LANES, (c + 1) * LANES)
        w = [jnp.zeros((PEER_NKEYS, LANES), BF16) for _ in range(GATE_KEYS)]
        for h in range(PEER_HEADS):
            r1 = r1s_ref[h, :, sl]
            p1 = p1s_ref[h, :, sl]
            for k in range(GATE_KEYS):
                n_i = n_ref[h, row0 + i0 + k:row0 + i0 + k + 1, sl]
                p0 = p0_ref[h, row0 + i0 + k:row0 + i0 + k + 1, sl]
                w[k] = w[k] + jnp.where(r1 < n_i, p1 * p0, jnp.zeros_like(p1))
        for k in range(GATE_KEYS):
            rows = slice((i0 + k) * PEER_NKEYS, (i0 + k + 1) * PEER_NKEYS)
            a = a_ref[slot, rows, sl]
            gelu = 0.5 * a * (1.0 + lax.erf(a * (2.0 ** -0.5)))
            g_ref[slot, rows, sl] = (w[k].astype(F32) * gelu).astype(BF16)

    def combine(vT_ref, slot, piece, pieces):
        rows = pl.ds(piece * (d_model // pieces), d_model // pieces)
        acc_ref[rows, :] += jnp.dot(vT_ref[rows, :], g_ref[slot], preferred_element_type=F32)

    def half_trip(u_ref, n_ref, p0_ref, row0, vT_ref, gate_slot):
        per_piece = len(gate_blocks) // MXU_PIECES
        for piece in range(MXU_PIECES):
            project(u_ref, 1 - gate_slot, piece, MXU_PIECES)
            for c, i0 in gate_blocks[piece * per_piece:(piece + 1) * per_piece]:
                gate_block(n_ref, p0_ref, row0, gate_slot, c, i0)
        pl.when(step >= 0)(lambda: combine(vT_ref, 1 - gate_slot, 0, 1))

    @pl.when(step == 0)
    def _():
        acc_ref[...] = jnp.zeros_like(acc_ref)
        g_ref[0] = jnp.zeros(g_ref.shape[1:], BF16)
        a_ref[1] = jnp.zeros(a_ref.shape[1:], F32)
        r1s_ref[...] = r1_ref[...]
        p1s_ref[...] = p1_ref[...]

    @pl.when(step < n_trips)
    def _():
        half_trip(u_a_ref, n_a_ref, p0_a_ref, IG_PEER, vT_a_ref, 1)
        half_trip(u_b_ref, n_b_ref, p0_b_ref, 0, vT_b_ref, 0)

    @pl.when(step == n_trips)
    def _():
        for c, i0 in gate_blocks:
            gate_block(n_a_ref, p0_a_ref, IG_PEER, 1, c, i0)
        combine(vT_a_ref, 0, 0, 1)
        combine(vT_b_ref, 1, 0, 1)
        y = x1_ref[...] + acc_ref[...].T
        ms = jnp.mean(y * y, axis=-1, keepdims=True)
        out_ref[...] = y * lax.rsqrt(ms + EPS) * fn_ref[...]


def _peer_dense(h2T, r1, p1, n, p0, peer_u, peer_v, x1, final_norm):
    D, N = h2T.shape
    tm = TM_PEER
    n_exp = peer_u.shape[0]
    u = peer_u.astype(BF16)
    vT = peer_v.T.astype(BF16)
    rows = IG_PEER * PEER_NKEYS
    ng = n_exp // rows
    assert ng % 2 == 0
    last = ng - 1
    rt = pl.BlockSpec((PEER_HEADS, PEER_NKEYS, tm), lambda t, s: (0, 0, t))
    grp = lambda f: pl.BlockSpec((PEER_HEADS, None, 2 * IG_PEER, tm),
                                 lambda t, s: (0, jnp.clip(f(s), 0, ng // 2 - 1), 0, t))
    u_spec = lambda f: pl.BlockSpec((rows, D), lambda t, s: (jnp.clip(f(s), 0, last), 0))
    v_spec = lambda f: pl.BlockSpec((D, rows), lambda t, s: (0, jnp.clip(f(s), 0, last)))
    n = n.reshape(PEER_HEADS, ng // 2, 2 * IG_PEER, N)
    p0 = p0.reshape(PEER_HEADS, ng // 2, 2 * IG_PEER, N)
    return pl.pallas_call(
        functools.partial(_peer_dense_kernel, n_trips=ng // 2),
        grid=(N // tm, ng // 2 + 1),
        in_specs=[pl.BlockSpec((D, tm), lambda t, s: (0, t)), rt, rt,
                  grp(lambda s: s - 1), grp(lambda s: s), grp(lambda s: s - 1), grp(lambda s: s),
                  u_spec(lambda s: 2 * s), u_spec(lambda s: 2 * s + 1),
                  v_spec(lambda s: 2 * s - 2), v_spec(lambda s: 2 * s - 1),
                  pl.BlockSpec((tm, D), lambda t, s: (t, 0)),
                  pl.BlockSpec((1, D), lambda t, s: (0, 0))],
        out_specs=pl.BlockSpec((tm, D), lambda t, s: (t, 0)),
        out_shape=jax.ShapeDtypeStruct((N, D), F32),
        scratch_shapes=[pltpu.VMEM((D, tm), F32), pltpu.VMEM((2, rows, tm), F32), pltpu.VMEM((2, rows, tm), BF16),
                        pltpu.VMEM((PEER_HEADS, PEER_NKEYS, tm), BF16), pltpu.VMEM((PEER_HEADS, PEER_NKEYS, tm), BF16)],
        compiler_params=_params(("parallel", "arbitrary")),
        name="peer_dense",
    )(h2T, r1, p1, n, n, p0, p0, u, u, vT, vT, x1, final_norm.reshape(1, D))


def kernel(x, attn_norm, w_in, ret_gn, w_ret_o, w_dsa_o, w_out, ffn_norm, peer_wq, peer_subkeys, peer_u, peer_v,
           final_norm):
    B, L, D = x.shape
    assert attn_norm.shape[0] == 1, "single-layer block"
    assert L % TM_IN == 0 and (B * L) % TM_PEER == 0 and D % LANES == 0 and QB == LANES
    rq, rk, rv, rg, dqT, dk, dvT, iqT, ik, iwT, gates = _inproj(x, attn_norm[0], w_in[0])
    y_ret = _retention(rq, rk, rv, rg, ret_gn[0])
    y_dsa = _dsa(iqT, iwT, ik, dqT, dk, dvT)
    x1, h2T = _merge(y_ret.reshape(B * L, RET_V), y_dsa.reshape(B * L, DSA_W), gates.reshape(B * L, 2 * D),
                     x.reshape(B * L, D), w_ret_o[0], w_dsa_o[0], w_out[0], ffn_norm[0])
    r1, p1, n, p0 = _peer_route(h2T, peer_wq[0], peer_subkeys[0])
    out = _peer_dense(h2T, r1, p1, n, p0, peer_u[0], peer_v[0], x1, final_norm)
    return out.reshape(B, L, D)
```

```python
import functools

import jax
import jax.numpy as jnp
import numpy as np
from jax import lax
from jax.experimental import pallas as pl
from jax.experimental.pallas import tpu as pltpu

F32 = jnp.float32
BF16 = jnp.bfloat16
I32 = jnp.int32

CHUNK = 64
EPS = 1e-6
RET_HEADS, RET_DK, RET_DV, RET_THETA = 4, 64, 128, 10000.0
DSA_HEADS, DSA_DH, IDX_HEADS, IDX_DH = 8, 64, 4, 64
DSA_TOPK_MAX = 256
ROPE_THETA, ROPE_DIM = 500000.0, 16
PEER_HEADS, PEER_NKEYS, PEER_DKEY, PEER_TOPK = 8, 128, 256, 16

LANES = 128
VMEM_LIMIT = 56 * 1024 * 1024

RET_QK = RET_HEADS * RET_DK
RET_V = RET_HEADS * RET_DV
DSA_W = DSA_HEADS * DSA_DH
IDX_Q = IDX_HEADS * IDX_DH

TM_IN = 512
TB_RET = 256
QB = 128
KB = 256
SM_ROWS = 64
TM_MERGE = 512
TM_ROUTE = 256
TM_PEER = 512
IG_PEER = 16

NEG_BIG = -1e30
M_INIT = -1e20
INT_MIN = -(2 ** 31)

_OFF = {}
_o = 0
for _name, _w in (("rq", RET_QK), ("rk", RET_QK), ("rv", RET_V), ("rg", RET_V), ("dq", DSA_W), ("dk", DSA_W),
                  ("dv", DSA_W), ("iq", IDX_Q), ("ik", LANES), ("iw", LANES), ("gates", None)):
    _OFF[_name] = _o
    if _w is not None:
        _o += _w


def _params(sem):
    return pltpu.CompilerParams(dimension_semantics=sem, vmem_limit_bytes=VMEM_LIMIT)


def _rope_chunk(p, cos, sin, half):
    lane = lax.broadcasted_iota(I32, p.shape, 1) % 64
    partner = jnp.where(lane < half, pltpu.roll(p, LANES - half, 1), pltpu.roll(p, half, 1))
    return p * cos + partner * sin


def _inproj_kernel(x_ref, g_ref, w_ref, cr_ref, sr_ref, cp_ref, sp_ref,
                   rq_ref, rk_ref, rv_ref, rg_ref, dqT_ref, dk_ref, dvT_ref, iqT_ref, ik_ref, iwT_ref, gate_ref,
                   *, d_model):
    x = x_ref[...]
    ms = jnp.mean(x * x, axis=-1, keepdims=True)
    h = (x * lax.rsqrt(ms + EPS) * g_ref[...]).astype(BF16)

    def proj(off, width):
        return jnp.dot(h, w_ref[:, off:off + width], preferred_element_type=F32)

    cr, sr, cp, sp = cr_ref[...], sr_ref[...], cp_ref[...], sp_ref[...]
    qk_scale = RET_DK ** -0.5
    for c in range(RET_QK // LANES):
        sl = slice(c * LANES, (c + 1) * LANES)
        rq_ref[:, sl] = _rope_chunk(proj(_OFF["rq"] + c * LANES, LANES), cr, sr, RET_DK // 2) * qk_scale
        rk_ref[:, sl] = _rope_chunk(proj(_OFF["rk"] + c * LANES, LANES), cr, sr, RET_DK // 2)
    rv_ref[...] = proj(_OFF["rv"], RET_V).astype(BF16)
    rg_ref[...] = proj(_OFF["rg"], RET_V)
    tm = x.shape[0]
    dh_scale = DSA_DH ** -0.5
    for c in range(DSA_W // LANES):
        sl = slice(c * LANES, (c + 1) * LANES)
        dq = _rope_chunk(proj(_OFF["dq"] + c * LANES, LANES), cp, sp, ROPE_DIM // 2) * dh_scale
        dqT_ref[sl, :] = dq.T.astype(BF16)
        dk_ref[:, sl] = _rope_chunk(proj(_OFF["dk"] + c * LANES, LANES), cp, sp, ROPE_DIM // 2).astype(BF16)
        dv = proj(_OFF["dv"] + c * LANES, LANES)
        for j in range(tm // KB):
            dvT_ref[j, sl, :] = dv[j * KB:(j + 1) * KB, :].T.astype(BF16)
    for c in range(IDX_Q // LANES):
        iq = _rope_chunk(proj(_OFF["iq"] + c * LANES, LANES), cp, sp, ROPE_DIM // 2)
        for j in range(tm // QB):
            t = iq[j * QB:(j + 1) * QB, :].T.astype(BF16)
            iqT_ref[j, :, (2 * c) * QB:(2 * c + 1) * QB] = t[:IDX_DH, :]
            iqT_ref[j, :, (2 * c + 1) * QB:(2 * c + 2) * QB] = t[IDX_DH:, :]
    ik = _rope_chunk(proj(_OFF["ik"], LANES), cp, sp, ROPE_DIM // 2)
    ik_ref[...] = ik[:, :IDX_DH].astype(BF16)
    iw = proj(_OFF["iw"], LANES) * ((IDX_HEADS ** -0.5) * (IDX_DH ** -0.5))
    iwT_ref[...] = iw.T[:8, :]
    gate_ref[...] = proj(_OFF["gates"], 2 * d_model)


def _rope_tables(L, half, theta, pass_dims):
    inv = theta ** (-jnp.arange(half, dtype=F32) / half)
    ang = jnp.arange(L).astype(F32)[:, None] * inv[None, :]
    cos, sin = jnp.cos(ang), jnp.sin(ang)
    ones = jnp.ones((L, pass_dims), F32)
    zeros = jnp.zeros((L, pass_dims), F32)
    cos64 = jnp.concatenate([cos, cos, ones], axis=-1)
    sin64 = jnp.concatenate([-sin, sin, zeros], axis=-1)
    return jnp.tile(cos64, (1, 2)), jnp.tile(sin64, (1, 2))


def _inproj(x, attn_norm, w_in):
    B, L, D = x.shape
    tm = TM_IN
    nl = L // tm
    sizes = (RET_QK, RET_QK, RET_V, RET_V, DSA_W, DSA_W, DSA_W, IDX_Q, IDX_DH, IDX_HEADS, 2 * D)
    offs = np.concatenate([[0], np.cumsum(sizes)])
    cols = [w_in[:, offs[i]:offs[i + 1]] for i in range(len(sizes))]
    cols[8] = jnp.pad(cols[8], ((0, 0), (0, LANES - IDX_DH)))
    cols[9] = jnp.pad(cols[9], ((0, 0), (0, LANES - IDX_HEADS)))
    w = jnp.concatenate(cols, axis=1).astype(BF16)
    wtot = w.shape[1]
    cr, sr = _rope_tables(L, RET_DK // 2, RET_THETA, 0)
    cp, sp = _rope_tables(L, ROPE_DIM // 2, ROPE_THETA, DSA_DH - ROPE_DIM)

    tok = lambda width: pl.BlockSpec((None, tm, width), lambda b, l: (b, l, 0))
    tab = pl.BlockSpec((tm, LANES), lambda b, l: (l, 0))
    out_shape = (
        jax.ShapeDtypeStruct((B, L, RET_QK), F32),
        jax.ShapeDtypeStruct((B, L, RET_QK), F32),
        jax.ShapeDtypeStruct((B, L, RET_V), BF16),
        jax.ShapeDtypeStruct((B, L, RET_V), F32),
        jax.ShapeDtypeStruct((B, DSA_W, L), BF16),
        jax.ShapeDtypeStruct((B, L, DSA_W), BF16),
        jax.ShapeDtypeStruct((B, L // KB, DSA_W, KB), BF16),
        jax.ShapeDtypeStruct((B, L // QB, IDX_DH, IDX_HEADS * QB), BF16),
        jax.ShapeDtypeStruct((B, L, IDX_DH), BF16),
        jax.ShapeDtypeStruct((B, 8, L), F32),
        jax.ShapeDtypeStruct((B, L, 2 * D), F32),
    )
    out_specs = (
        tok(RET_QK), tok(RET_QK), tok(RET_V), tok(RET_V),
        pl.BlockSpec((None, DSA_W, tm), lambda b, l: (b, 0, l)),
        tok(DSA_W),
        pl.BlockSpec((None, tm // KB, DSA_W, KB), lambda b, l: (b, l, 0, 0)),
        pl.BlockSpec((None, tm // QB, IDX_DH, IDX_HEADS * QB), lambda b, l: (b, l, 0, 0)),
        tok(IDX_DH),
        pl.BlockSpec((None, 8, tm), lambda b, l: (b, 0, l)),
        tok(2 * D),
    )
    return pl.pallas_call(
        functools.partial(_inproj_kernel, d_model=D),
        grid=(B, nl),
        in_specs=[tok(D), pl.BlockSpec((1, D), lambda b, l: (0, 0)),
                  pl.BlockSpec((D, wtot), lambda b, l: (0, 0)), tab, tab, tab, tab],
        out_specs=out_specs,
        out_shape=out_shape,
        compiler_params=_params(("parallel", "parallel")),
        name="inproj",
    )(x, attn_norm.reshape(1, D), w, cr, sr, cp, sp)


def _retention_kernel(q_ref, k_ref, v_ref, g_ref, gn_ref, dm_ref, qd_ref, kd_ref, cd_ref, y_ref, state_ref):
    @pl.when(pl.program_id(1) == 0)
    def _():
        state_ref[...] = jnp.zeros_like(state_ref)

    for h in range(RET_HEADS):
        q = q_ref[:, h * RET_DK:(h + 1) * RET_DK]
        k = k_ref[:, h * RET_DK:(h + 1) * RET_DK]
        v = v_ref[:, h * RET_DV:(h + 1) * RET_DV]
        s = lax.dot_general(q.astype(BF16), k.astype(BF16), (((1,), (1,)), ((), ())), preferred_element_type=F32)
        intra = jnp.dot((s * dm_ref[h]).astype(BF16), v, preferred_element_type=F32)
        state = state_ref[h]
        cross = jnp.dot((q * qd_ref[h]).astype(BF16), state.astype(BF16), preferred_element_type=F32)
        kv = lax.dot_general((k * kd_ref[h]).astype(BF16), v, (((0,), (0,)), ((), ())), preferred_element_type=F32)
        state_ref[h] = cd_ref[h] * state + kv
        y = intra + cross
        mu = jnp.mean(y, axis=-1, keepdims=True)
        yc = y - mu
        var = jnp.mean(yc * yc, axis=-1, keepdims=True)
        sl = slice(h * RET_DV, (h + 1) * RET_DV)
        yn = yc * lax.rsqrt(var + EPS) * gn_ref[:, sl]
        g = g_ref[:, sl]
        y_ref[:, sl] = (g * jax.nn.sigmoid(g) * yn).astype(y_ref.dtype)


def _retention(rq, rk, rv, rg, ret_gn):
    B, L, _ = rq.shape
    tb = TB_RET
    log_g = jnp.log(1.0 - 2.0 ** (-5.0 - jnp.arange(RET_HEADS, dtype=F32)))
    i = jnp.arange(tb, dtype=F32)
    ci = jnp.arange(tb) // CHUNK
    visible = (ci[None, :] <= ci[:, None]).astype(F32)
    dm = jnp.exp(log_g[:, None, None] * jnp.abs(i[:, None] - i[None, :])) * visible[None]
    ones = jnp.ones((1, 1, RET_DK), F32)
    qd = jnp.exp(log_g[:, None] * (i + 1.0)[None, :])[:, :, None] * ones
    kd = jnp.exp(log_g[:, None] * (tb - 1.0 - i)[None, :])[:, :, None] * ones
    cd = jnp.exp(log_g * tb)[:, None, None] * jnp.ones((1, RET_DK, RET_DV), F32)

    tok = lambda width: pl.BlockSpec((None, tb, width), lambda b, l: (b, l, 0))
    full = lambda shape: pl.BlockSpec(shape, lambda b, l: (0,) * len(shape))
    return pl.pallas_call(
        _retention_kernel,
        grid=(B, L // tb),
        in_specs=[tok(RET_QK), tok(RET_QK), tok(RET_V), tok(RET_V), full((1, RET_V)),
                  full((RET_HEADS, tb, tb)), full((RET_HEADS, tb, RET_DK)), full((RET_HEADS, tb, RET_DK)),
                  full((RET_HEADS, RET_DK, RET_DV))],
        out_specs=tok(RET_V),
        out_shape=jax.ShapeDtypeStruct((B, L, RET_V), BF16),
        scratch_shapes=[pltpu.VMEM((RET_HEADS, RET_DK, RET_DV), F32)],
        compiler_params=_params(("parallel", "arbitrary")),
        name="retention",
    )(rq, rk, rv, rg, ret_gn.reshape(1, RET_V), dm, qd, kd, cd)


def _dsa_kernel(iqT_ref, iwT_ref, ik_ref, dqT_ref, dk_ref, dvT_ref, tri_ref, y_ref,
                keys_ref, rhs_ref, m_ref, l_ref, acc_ref, s_ref, p_ref, alpha_ref, smax_ref, *, topk):
    qb = pl.program_id(1)
    nkb = (qb * QB) // KB + 1
    n_trips = (nkb + 1) // 2
    n_pairs = DSA_HEADS // 2

    key_row = lax.broadcasted_iota(I32, (KB, QB), 0)
    q_chunk = (qb * QB + lax.broadcasted_iota(I32, (KB, QB), 1)) // CHUNK

    def score_block(kb):
        k0 = pl.multiple_of(kb * KB, KB)
        r = jnp.dot(ik_ref[pl.ds(k0, KB), :], iqT_ref[...], preferred_element_type=F32)
        acc = jnp.zeros((KB, QB), F32)
        for h in range(IDX_HEADS):
            acc = acc + iwT_ref[h:h + 1, :] * jnp.maximum(r[:, h * QB:(h + 1) * QB], 0.0)
        acc = acc + 0.0
        bits = pltpu.bitcast(acc, I32)
        key = jnp.where(bits < 0, bits ^ 0x7FFFFFFF, bits)
        admissible = (k0 + key_row) // CHUNK <= q_chunk
        keys_ref[kb] = jnp.where(admissible, key, INT_MIN)

    def score_two_blocks(t, carry):
        score_block(2 * t)
        score_block(2 * t + 1)
        return carry

    lax.fori_loop(0, n_trips, score_two_blocks, 0)

    def count_ge(cand):
        def body(t, cnt):
            for kb in (2 * t, 2 * t + 1):
                hit = jnp.where(keys_ref[kb] >= cand, 1.0, 0.0)
                cnt = cnt + jnp.sum(hit.reshape(KB // SM_ROWS, SM_ROWS, QB), axis=0)
            return cnt

        cnt = lax.fori_loop(0, n_trips, body, jnp.zeros((SM_ROWS, QB), F32))
        return jnp.sum(cnt, axis=0, keepdims=True).astype(I32)

    zero = jnp.zeros((1, QB), I32)
    t0 = jnp.where(count_ge(zero) >= topk, zero, zero + INT_MIN)

    def bit_step(n, t):
        cand = t + jnp.left_shift(jnp.int32(1), 30 - n)
        return jnp.where(count_ge(cand) >= topk, cand, t)

    thr = lax.fori_loop(0, 31, bit_step, t0)
    n_gt = count_ge(thr + 1)
    need = jnp.where(thr == INT_MIN, 0, topk - n_gt).astype(F32)

    d_row = lax.broadcasted_iota(I32, (LANES, QB), 0)
    eye = jnp.where(d_row == lax.broadcasted_iota(I32, (LANES, QB), 1), 1.0, 0.0).astype(BF16)
    for j in range(n_pairs):
        qT = dqT_ref[j * LANES:(j + 1) * LANES, :]
        rhs_ref[j, :LANES, :QB] = jnp.where(d_row < DSA_DH, qT, jnp.zeros_like(qT))
        rhs_ref[j, :LANES, QB:] = jnp.where(d_row >= DSA_DH, qT, jnp.zeros_like(qT))
        rhs_ref[j, LANES:, :QB] = eye
        rhs_ref[j, LANES:, QB:] = eye
    m_ref[...] = jnp.full(m_ref.shape, M_INIT, F32)
    l_ref[...] = jnp.zeros(l_ref.shape, F32)
    acc_ref[...] = jnp.zeros(acc_ref.shape, F32)

    def logits(blk, slot, n_eq_before):
        kb = jnp.minimum(blk, nkb - 1)
        k0 = pl.multiple_of(kb * KB, KB)
        key = jnp.where(blk < nkb, keys_ref[kb], INT_MIN)
        eq = key == thr
        eq_f = jnp.where(eq, 1.0, 0.0)
        prefix = jnp.dot(tri_ref[...], eq_f.astype(BF16), preferred_element_type=F32)
        keep_tie = (n_eq_before + prefix) < need
        bias = jnp.where(key > thr, 0.0, jnp.where(eq, jnp.where(keep_tie, 0.0, NEG_BIG), NEG_BIG)).astype(BF16)
        for j in range(n_pairs):
            lhs = jnp.concatenate([dk_ref[pl.ds(k0, KB), j * LANES:(j + 1) * LANES], bias], axis=1)
            s = jnp.dot(lhs, rhs_ref[j], preferred_element_type=F32)
            s_ref[slot, j] = s
            smax_ref[slot, j] = jnp.max(s, axis=0, keepdims=True)
        return n_eq_before + jnp.sum(eq_f, axis=0, keepdims=True)

    def softmax_update(slot):
        rows = [slice(r * SM_ROWS, (r + 1) * SM_ROWS) for r in range(KB // SM_ROWS)]
        for j in range(n_pairs):
            m_prev = m_ref[j]
            m_new = jnp.maximum(m_prev, smax_ref[slot, j])
            alpha = jnp.exp(m_prev - m_new)
            psum = jnp.zeros((SM_ROWS, 2 * QB), F32)
            for r in rows:
                p = jnp.exp(s_ref[slot, j, r, :] - m_new)
                psum = psum + p
                p_ref[slot, j, r, :] = p.astype(BF16)
            l_ref[j] = alpha * l_ref[j] + jnp.sum(psum, axis=0, keepdims=True)
            m_ref[j] = m_new
            alpha_ref[slot, j] = alpha

    def accumulate(kb, slot):
        for j in range(n_pairs):
            pv = jnp.dot(dvT_ref[kb, j * LANES:(j + 1) * LANES, :], p_ref[slot, j], preferred_element_type=F32)
            acc_ref[j] = alpha_ref[slot, j] * acc_ref[j] + pv

    p_ref[1] = jnp.zeros(p_ref.shape[1:], BF16)
    alpha_ref[1] = jnp.ones(alpha_ref.shape[1:], F32)
    n_eq0 = logits(0, 0, jnp.zeros((1, QB), F32))

    def attend_two_blocks(t, n_eq):
        b = 2 * t
        n_eq = logits(b + 1, 1, n_eq)
        softmax_update(0)
        accumulate(jnp.maximum(b - 1, 0), 1)
        n_eq = logits(b + 2, 0, n_eq)
        softmax_update(1)
        accumulate(b, 0)
        return n_eq

    lax.fori_loop(0, n_trips, attend_two_blocks, n_eq0)
    accumulate(jnp.minimum(2 * n_trips - 1, nkb - 1), 1)

    outs = []
    for j in range(n_pairs):
        acc, l = acc_ref[j], l_ref[j]
        outs.append(acc[:DSA_DH, :QB] / l[:, :QB])
        outs.append(acc[DSA_DH:, QB:] / l[:, QB:])
    y_ref[...] = jnp.concatenate(outs, axis=0).T.astype(y_ref.dtype)


def _dsa(iqT, iwT, ik, dqT, dk, dvT):
    B, L, _ = dk.shape
    topk = min(DSA_TOPK_MAX, L // 4)
    r = jnp.arange(KB)
    tri = (r[None, :] < r[:, None]).astype(BF16)
    per_batch = lambda shape: pl.BlockSpec((None,) + shape, lambda b, q: (b,) + (0,) * len(shape))
    return pl.pallas_call(
        functools.partial(_dsa_kernel, topk=topk),
        grid=(B, L // QB),
        in_specs=[pl.BlockSpec((None, None, IDX_DH, IDX_HEADS * QB), lambda b, q: (b, q, 0, 0)),
                  pl.BlockSpec((None, 8, QB), lambda b, q: (b, 0, q)),
                  per_batch((L, IDX_DH)),
                  pl.BlockSpec((None, DSA_W, QB), lambda b, q: (b, 0, q)),
                  per_batch((L, DSA_W)),
                  per_batch((L // KB, DSA_W, KB)),
                  pl.BlockSpec((KB, KB), lambda b, q: (0, 0))],
        out_specs=pl.BlockSpec((None, QB, DSA_W), lambda b, q: (b, q, 0)),
        out_shape=jax.ShapeDtypeStruct((B, L, DSA_W), BF16),
        scratch_shapes=[pltpu.VMEM((L // KB, KB, QB), I32),
                        pltpu.VMEM((DSA_HEADS // 2, 2 * LANES, 2 * QB), BF16),
                        pltpu.VMEM((DSA_HEADS // 2, 1, 2 * QB), F32),
                        pltpu.VMEM((DSA_HEADS // 2, 1, 2 * QB), F32),
                        pltpu.VMEM((DSA_HEADS // 2, LANES, 2 * QB), F32),
                        pltpu.VMEM((2, DSA_HEADS // 2, KB, 2 * QB), F32),
                        pltpu.VMEM((2, DSA_HEADS // 2, KB, 2 * QB), BF16),
                        pltpu.VMEM((2, DSA_HEADS // 2, 1, 2 * QB), F32),
                        pltpu.VMEM((2, DSA_HEADS // 2, 1, 2 * QB), F32)],
        compiler_params=_params(("parallel", "arbitrary")),
        name="dsa",
    )(iqT, iwT, ik, dqT, dk, dvT, tri)


def _merge_kernel(yr_ref, yd_ref, gate_ref, x_ref, wr_ref, wd_ref, wo_ref, fn_ref, x1_ref, h2T_ref, *, d_model):
    g = gate_ref[...]
    br = jnp.dot(yr_ref[...], wr_ref[...], preferred_element_type=F32)
    bd = jnp.dot(yd_ref[...], wd_ref[...], preferred_element_type=F32)
    merged = jax.nn.sigmoid(g[:, :d_model]) * br + jax.nn.sigmoid(g[:, d_model:]) * bd
    x1 = x_ref[...] + jnp.dot(merged.astype(BF16), wo_ref[...], preferred_element_type=F32)
    x1_ref[...] = x1
    ms = jnp.mean(x1 * x1, axis=-1, keepdims=True)
    h2 = x1 * lax.rsqrt(ms + EPS) * fn_ref[...]
    h2T_ref[...] = h2.T.astype(BF16)


def _merge(y_ret, y_dsa, gates, x, w_ret_o, w_dsa_o, w_out, ffn_norm):
    N, D = x.shape
    tm = TM_MERGE
    tok = lambda width: pl.BlockSpec((tm, width), lambda t: (t, 0))
    full = lambda shape: pl.BlockSpec(shape, lambda t: (0, 0))
    return pl.pallas_call(
        functools.partial(_merge_kernel, d_model=D),
        grid=(N // tm,),
        in_specs=[tok(RET_V), tok(DSA_W), tok(2 * D), tok(D),
                  full((RET_V, D)), full((DSA_W, D)), full((D, D)), full((1, D))],
        out_specs=(tok(D), pl.BlockSpec((D, tm), lambda t: (0, t))),
        out_shape=(jax.ShapeDtypeStruct((N, D), F32), jax.ShapeDtypeStruct((D, N), BF16)),
        compiler_params=_params(("parallel",)),
        name="merge",
    )(y_ret, y_dsa, gates, x, w_ret_o.astype(BF16), w_dsa_o.astype(BF16), w_out.astype(BF16),
      ffn_norm.reshape(1, D))


_ROW_LEN = [PEER_TOPK // (a + 1) for a in range(PEER_TOPK)]


def _extract_rounds(s, n, break_ties):
    rows = lax.broadcasted_iota(I32, s.shape, 0).astype(F32)
    rank = jnp.full(s.shape, float(n), F32)
    vals = []
    for a in range(n):
        m = jnp.max(s, axis=0, keepdims=True)
        hit = s == m
        if break_ties:
            first = jnp.min(jnp.where(hit, rows, float(s.shape[0])), axis=0, keepdims=True)
            hit = rows == first
        s = jnp.where(hit, -jnp.inf, s)
        rank = jnp.where(hit, float(a), rank)
        vals.append(m)
    return vals, rank


def _extract_top_pair(s_a, s_b, n):
    fast = (_extract_rounds(s_a, n, break_ties=False), _extract_rounds(s_b, n, break_ties=False))
    taken = sum(jnp.sum(jnp.where(rank < float(n), 1.0, 0.0), axis=0, keepdims=True) for _, rank in fast)
    return lax.cond(jnp.max(taken) > float(2 * n),
                    lambda: (_extract_rounds(s_a, n, break_ties=True), _extract_rounds(s_b, n, break_ties=True)),
                    lambda: fast)


def _peer_route_kernel(h2T_ref, wqT_ref, sk_ref, r1_ref, p1_ref, n_ref, p0_ref):
    h2T = h2T_ref[...]
    K = PEER_TOPK
    for h in range(PEER_HEADS):
        sT = []
        for c in range(2):
            r0 = (h * 2 + c) * PEER_NKEYS
            qT = jnp.dot(wqT_ref[r0:r0 + PEER_NKEYS, :], h2T, preferred_element_type=F32)
            sT.append(jnp.dot(sk_ref[h, c], qT.astype(BF16), preferred_element_type=F32))
        (v0, rank0), (v1, rank1) = _extract_top_pair(sT[0], sT[1], K)
        cand = [[v0[a] + v1[b] for b in range(_ROW_LEN[a])] for a in range(K)]
        best, _ = _extract_rounds(jnp.concatenate([c for row in cand for c in row], axis=0), K, break_ties=True)
        tau = best[K - 1]
        z = jnp.zeros_like(tau)
        for k in range(K):
            z = z + jnp.exp(best[k] - best[0])
        gt = [sum(jnp.where(c > tau, 1.0, 0.0) for c in row) for row in cand]
        eq = [sum(jnp.where(c == tau, 1.0, 0.0) for c in row) for row in cand]
        need = float(K) - sum(gt)
        n = jnp.zeros(rank0.shape, F32)
        for a in range(K):
            n_a = gt[a] + jnp.minimum(jnp.maximum(need, 0.0), eq[a])
            need = need - eq[a]
            n = jnp.where(rank0 == float(a), n_a, n)
        r1_ref[h] = rank1.astype(BF16)
        p1_ref[h] = jnp.exp(sT[1] - v1[0]).astype(BF16)
        n_ref[h] = n.astype(BF16)
        p0_ref[h] = (jnp.exp(sT[0] - v0[0]) / z).astype(BF16)


def _peer_route(h2T, peer_wq, peer_subkeys):
    D, N = h2T.shape
    tm = TM_ROUTE
    wqT = peer_wq.T.astype(BF16)
    sk = peer_subkeys.astype(BF16)
    spec = pl.BlockSpec((PEER_HEADS, PEER_NKEYS, tm), lambda t: (0, 0, t))
    shape = lambda dt: jax.ShapeDtypeStruct((PEER_HEADS, PEER_NKEYS, N), dt)
    return pl.pallas_call(
        _peer_route_kernel,
        grid=(N // tm,),
        in_specs=[pl.BlockSpec((D, tm), lambda t: (0, t)),
                  pl.BlockSpec(wqT.shape, lambda t: (0, 0)),
                  pl.BlockSpec(sk.shape, lambda t: (0, 0, 0, 0))],
        out_specs=(spec, spec, spec, spec),
        out_shape=(shape(BF16), shape(BF16), shape(BF16), shape(BF16)),
        compiler_params=_params(("parallel",)),
        name="peer_route",
    )(h2T, wqT, sk)


def _peer_dense_kernel(h2T_ref, r1_ref, p1_ref, n_ref, p0_ref, u_ref, vT_ref, x1_ref, fn_ref, out_ref,
                       acc_ref, a_ref, g_ref, r1s_ref, p1s_ref, *, n_groups):
    step = pl.program_id(1)
    tm = h2T_ref.shape[1]

    def project(slot):
        a_ref[slot] = jnp.dot(u_ref[...], h2T_ref[...], preferred_element_type=F32)

    def gate(slot):
        for c in range(tm // LANES):
            sl = slice(c * LANES, (c + 1) * LANES)
            for il in range(IG_PEER):
                rows = slice(il * PEER_NKEYS, (il + 1) * PEER_NKEYS)
                w = jnp.zeros((PEER_NKEYS, LANES), BF16)
                for h in range(PEER_HEADS):
                    n_i = n_ref[h, il:il + 1, sl]
                    p0 = p0_ref[h, il:il + 1, sl]
                    w = w + jnp.where(r1s_ref[h, :, sl] < n_i, p1s_ref[h, :, sl] * p0, jnp.zeros_like(w))
                a = a_ref[slot, rows, sl]
                gelu = 0.5 * a * (1.0 + lax.erf(a * (2.0 ** -0.5)))
                g_ref[slot, rows, sl] = (w.astype(F32) * gelu).astype(BF16)

    def combine(slot):
        acc_ref[...] += jnp.dot(vT_ref[...], g_ref[slot], preferred_element_type=F32)

    @pl.when(step == 0)
    def _():
        acc_ref[...] = jnp.zeros_like(acc_ref)
        g_ref[1] = jnp.zeros(g_ref.shape[1:], BF16)
        r1s_ref[...] = r1_ref[...]
        p1s_ref[...] = p1_ref[...]
        project(0)

    @pl.when(jnp.logical_and(step > 0, step <= n_groups))
    def _():
        project(step % 2)
        gate((step - 1) % 2)
        combine(step % 2)

    @pl.when(step == n_groups + 1)
    def _():
        combine(step % 2)
        y = x1_ref[...] + acc_ref[...].T
        ms = jnp.mean(y * y, axis=-1, keepdims=True)
        out_ref[...] = y * lax.rsqrt(ms + EPS) * fn_ref[...]


def _peer_dense(h2T, r1, p1, n, p0, peer_u, peer_v, x1, final_norm):
    D, N = h2T.shape
    tm = TM_PEER
    n_exp = peer_u.shape[0]
    u = peer_u.astype(BF16)
    vT = peer_v.T.astype(BF16)
    rows = IG_PEER * PEER_NKEYS
    ng = n_exp // rows
    rt = pl.BlockSpec((PEER_HEADS, PEER_NKEYS, tm), lambda t, s: (0, 0, t))
    grp = pl.BlockSpec((PEER_HEADS, None, IG_PEER, tm), lambda t, s: (0, jnp.clip(s - 1, 0, ng - 1), 0, t))
    n = n.reshape(PEER_HEADS, ng, IG_PEER, N)
    p0 = p0.reshape(PEER_HEADS, ng, IG_PEER, N)
    return pl.pallas_call(
        functools.partial(_peer_dense_kernel, n_groups=ng),
        grid=(N // tm, ng + 2),
        in_specs=[pl.BlockSpec((D, tm), lambda t, s: (0, t)), rt, rt, grp, grp,
                  pl.BlockSpec((rows, D), lambda t, s: (jnp.minimum(s, ng - 1), 0)),
                  pl.BlockSpec((D, rows), lambda t, s: (0, jnp.clip(s - 2, 0, ng - 1))),
                  pl.BlockSpec((tm, D), lambda t, s: (t, 0)),
                  pl.BlockSpec((1, D), lambda t, s: (0, 0))],
        out_specs=pl.BlockSpec((tm, D), lambda t, s: (t, 0)),
        out_shape=jax.ShapeDtypeStruct((N, D), F32),
        scratch_shapes=[pltpu.VMEM((D, tm), F32), pltpu.VMEM((2, rows, tm), F32), pltpu.VMEM((2, rows, tm), BF16),
                        pltpu.VMEM((PEER_HEADS, PEER_NKEYS, tm), BF16), pltpu.VMEM((PEER_HEADS, PEER_NKEYS, tm), BF16)],
        compiler_params=_params(("parallel", "arbitrary")),
        name="peer_dense",
    )(h2T, r1, p1, n, p0, u, vT, x1, final_norm.reshape(1, D))


def kernel(x, attn_norm, w_in, ret_gn, w_ret_o, w_dsa_o, w_out, ffn_norm, peer_wq, peer_subkeys, peer_u, peer_v,
           final_norm):
    B, L, D = x.shape
    assert attn_norm.shape[0] == 1, "single-layer block"
    assert L % TM_IN == 0 and (B * L) % TM_PEER == 0 and D % LANES == 0 and QB == LANES
    rq, rk, rv, rg, dqT, dk, dvT, iqT, ik, iwT, gates = _inproj(x, attn_norm[0], w_in[0])
    y_ret = _retention(rq, rk, rv, rg, ret_gn[0])
    y_dsa = _dsa(iqT, iwT, ik, dqT, dk, dvT)
    x1, h2T = _merge(y_ret.reshape(B * L, RET_V), y_dsa.reshape(B * L, DSA_W), gates.reshape(B * L, 2 * D),
                     x.reshape(B * L, D), w_ret_o[0], w_dsa_o[0], w_out[0], ffn_norm[0])
    r1, p1, n, p0 = _peer_route(h2T, peer_wq[0], peer_subkeys[0])
    out = _peer_dense(h2T, r1, p1, n, p0, peer_u[0], peer_v[0], x1, final_norm)
    return out.reshape(B, L, D)
```

```python
import functools

import jax
import jax.numpy as jnp
import numpy as np
from jax import lax
from jax.experimental import pallas as pl
from jax.experimental.pallas import tpu as pltpu

F32 = jnp.float32
BF16 = jnp.bfloat16
I32 = jnp.int32

CHUNK = 64
EPS = 1e-6
RET_HEADS, RET_DK, RET_DV, RET_THETA = 4, 64, 128, 10000.0
DSA_HEADS, DSA_DH, IDX_HEADS, IDX_DH = 8, 64, 4, 64
DSA_TOPK_MAX = 256
ROPE_THETA, ROPE_DIM = 500000.0, 16
PEER_HEADS, PEER_NKEYS, PEER_DKEY, PEER_TOPK = 8, 128, 256, 16

LANES = 128
VMEM_LIMIT = 56 * 1024 * 1024

RET_QK = RET_HEADS * RET_DK
RET_V = RET_HEADS * RET_DV
DSA_W = DSA_HEADS * DSA_DH
IDX_Q = IDX_HEADS * IDX_DH

TM_IN = 512
TB_RET = 256
QB = 128
KB = 256
SM_ROWS = 64
TM_MERGE = 512
TM_ROUTE = 256
TM_PEER = 512
IG_PEER = 16

NEG_BIG = -1e30
M_INIT = -1e20
INT_MIN = -(2 ** 31)

_OFF = {}
_o = 0
for _name, _w in (("rq", RET_QK), ("rk", RET_QK), ("rv", RET_V), ("rg", RET_V), ("dq", DSA_W), ("dk", DSA_W),
                  ("dv", DSA_W), ("iq", IDX_Q), ("ik", LANES), ("iw", LANES), ("gates", None)):
    _OFF[_name] = _o
    if _w is not None:
        _o += _w


def _params(sem):
    return pltpu.CompilerParams(dimension_semantics=sem, vmem_limit_bytes=VMEM_LIMIT)


def _rope_chunk(p, cos, sin, half):
    lane = lax.broadcasted_iota(I32, p.shape, 1) % 64
    partner = jnp.where(lane < half, pltpu.roll(p, LANES - half, 1), pltpu.roll(p, half, 1))
    return p * cos + partner * sin


def _inproj_kernel(x_ref, g_ref, w_ref, cr_ref, sr_ref, cp_ref, sp_ref,
                   rq_ref, rk_ref, rv_ref, rg_ref, dqT_ref, dk_ref, dvT_ref, iqT_ref, ik_ref, iwT_ref, gate_ref,
                   *, d_model):
    x = x_ref[...]
    ms = jnp.mean(x * x, axis=-1, keepdims=True)
    h = (x * lax.rsqrt(ms + EPS) * g_ref[...]).astype(BF16)

    def proj(off, width):
        return jnp.dot(h, w_ref[:, off:off + width], preferred_element_type=F32)

    cr, sr, cp, sp = cr_ref[...], sr_ref[...], cp_ref[...], sp_ref[...]
    qk_scale = RET_DK ** -0.5
    for c in range(RET_QK // LANES):
        sl = slice(c * LANES, (c + 1) * LANES)
        rq_ref[:, sl] = _rope_chunk(proj(_OFF["rq"] + c * LANES, LANES), cr, sr, RET_DK // 2) * qk_scale
        rk_ref[:, sl] = _rope_chunk(proj(_OFF["rk"] + c * LANES, LANES), cr, sr, RET_DK // 2)
    rv_ref[...] = proj(_OFF["rv"], RET_V).astype(BF16)
    rg_ref[...] = proj(_OFF["rg"], RET_V)
    tm = x.shape[0]
    dh_scale = DSA_DH ** -0.5
    for c in range(DSA_W // LANES):
        sl = slice(c * LANES, (c + 1) * LANES)
        dq = _rope_chunk(proj(_OFF["dq"] + c * LANES, LANES), cp, sp, ROPE_DIM // 2) * dh_scale
        dqT_ref[sl, :] = dq.T.astype(BF16)
        dk_ref[:, sl] = _rope_chunk(proj(_OFF["dk"] + c * LANES, LANES), cp, sp, ROPE_DIM // 2).astype(BF16)
        dv = proj(_OFF["dv"] + c * LANES, LANES)
        for j in range(tm // KB):
            dvT_ref[j, sl, :] = dv[j * KB:(j + 1) * KB, :].T.astype(BF16)
    for c in range(IDX_Q // LANES):
        iq = _rope_chunk(proj(_OFF["iq"] + c * LANES, LANES), cp, sp, ROPE_DIM // 2)
        for j in range(tm // QB):
            t = iq[j * QB:(j + 1) * QB, :].T.astype(BF16)
            iqT_ref[j, :, (2 * c) * QB:(2 * c + 1) * QB] = t[:IDX_DH, :]
            iqT_ref[j, :, (2 * c + 1) * QB:(2 * c + 2) * QB] = t[IDX_DH:, :]
    ik = _rope_chunk(proj(_OFF["ik"], LANES), cp, sp, ROPE_DIM // 2)
    ik_ref[...] = ik[:, :IDX_DH].astype(BF16)
    iw = proj(_OFF["iw"], LANES) * ((IDX_HEADS ** -0.5) * (IDX_DH ** -0.5))
    iwT_ref[...] = iw.T[:8, :]
    gate_ref[...] = proj(_OFF["gates"], 2 * d_model)


def _rope_tables(L, half, theta, pass_dims):
    inv = theta ** (-jnp.arange(half, dtype=F32) / half)
    ang = jnp.arange(L).astype(F32)[:, None] * inv[None, :]
    cos, sin = jnp.cos(ang), jnp.sin(ang)
    ones = jnp.ones((L, pass_dims), F32)
    zeros = jnp.zeros((L, pass_dims), F32)
    cos64 = jnp.concatenate([cos, cos, ones], axis=-1)
    sin64 = jnp.concatenate([-sin, sin, zeros], axis=-1)
    return jnp.tile(cos64, (1, 2)), jnp.tile(sin64, (1, 2))


def _inproj(x, attn_norm, w_in):
    B, L, D = x.shape
    tm = TM_IN
    nl = L // tm
    sizes = (RET_QK, RET_QK, RET_V, RET_V, DSA_W, DSA_W, DSA_W, IDX_Q, IDX_DH, IDX_HEADS, 2 * D)
    offs = np.concatenate([[0], np.cumsum(sizes)])
    cols = [w_in[:, offs[i]:offs[i + 1]] for i in range(len(sizes))]
    cols[8] = jnp.pad(cols[8], ((0, 0), (0, LANES - IDX_DH)))
    cols[9] = jnp.pad(cols[9], ((0, 0), (0, LANES - IDX_HEADS)))
    w = jnp.concatenate(cols, axis=1).astype(BF16)
    wtot = w.shape[1]
    cr, sr = _rope_tables(L, RET_DK // 2, RET_THETA, 0)
    cp, sp = _rope_tables(L, ROPE_DIM // 2, ROPE_THETA, DSA_DH - ROPE_DIM)

    tok = lambda width: pl.BlockSpec((None, tm, width), lambda b, l: (b, l, 0))
    tab = pl.BlockSpec((tm, LANES), lambda b, l: (l, 0))
    out_shape = (
        jax.ShapeDtypeStruct((B, L, RET_QK), F32),
        jax.ShapeDtypeStruct((B, L, RET_QK), F32),
        jax.ShapeDtypeStruct((B, L, RET_V), BF16),
        jax.ShapeDtypeStruct((B, L, RET_V), F32),
        jax.ShapeDtypeStruct((B, DSA_W, L), BF16),
        jax.ShapeDtypeStruct((B, L, DSA_W), BF16),
        jax.ShapeDtypeStruct((B, L // KB, DSA_W, KB), BF16),
        jax.ShapeDtypeStruct((B, L // QB, IDX_DH, IDX_HEADS * QB), BF16),
        jax.ShapeDtypeStruct((B, L, IDX_DH), BF16),
        jax.ShapeDtypeStruct((B, 8, L), F32),
        jax.ShapeDtypeStruct((B, L, 2 * D), F32),
    )
    out_specs = (
        tok(RET_QK), tok(RET_QK), tok(RET_V), tok(RET_V),
        pl.BlockSpec((None, DSA_W, tm), lambda b, l: (b, 0, l)),
        tok(DSA_W),
        pl.BlockSpec((None, tm // KB, DSA_W, KB), lambda b, l: (b, l, 0, 0)),
        pl.BlockSpec((None, tm // QB, IDX_DH, IDX_HEADS * QB), lambda b, l: (b, l, 0, 0)),
        tok(IDX_DH),
        pl.BlockSpec((None, 8, tm), lambda b, l: (b, 0, l)),
        tok(2 * D),
    )
    return pl.pallas_call(
        functools.partial(_inproj_kernel, d_model=D),
        grid=(B, nl),
        in_specs=[tok(D), pl.BlockSpec((1, D), lambda b, l: (0, 0)),
                  pl.BlockSpec((D, wtot), lambda b, l: (0, 0)), tab, tab, tab, tab],
        out_specs=out_specs,
        out_shape=out_shape,
        compiler_params=_params(("parallel", "parallel")),
        name="inproj",
    )(x, attn_norm.reshape(1, D), w, cr, sr, cp, sp)


def _retention_kernel(q_ref, k_ref, v_ref, g_ref, gn_ref, dm_ref, qd_ref, kd_ref, cd_ref, y_ref, state_ref):
    @pl.when(pl.program_id(1) == 0)
    def _():
        state_ref[...] = jnp.zeros_like(state_ref)

    for h in range(RET_HEADS):
        q = q_ref[:, h * RET_DK:(h + 1) * RET_DK]
        k = k_ref[:, h * RET_DK:(h + 1) * RET_DK]
        v = v_ref[:, h * RET_DV:(h + 1) * RET_DV]
        s = lax.dot_general(q.astype(BF16), k.astype(BF16), (((1,), (1,)), ((), ())), preferred_element_type=F32)
        intra = jnp.dot((s * dm_ref[h]).astype(BF16), v, preferred_element_type=F32)
        state = state_ref[h]
        cross = jnp.dot((q * qd_ref[h]).astype(BF16), state.astype(BF16), preferred_element_type=F32)
        kv = lax.dot_general((k * kd_ref[h]).astype(BF16), v, (((0,), (0,)), ((), ())), preferred_element_type=F32)
        state_ref[h] = cd_ref[h] * state + kv
        y = intra + cross
        mu = jnp.mean(y, axis=-1, keepdims=True)
        yc = y - mu
        var = jnp.mean(yc * yc, axis=-1, keepdims=True)
        sl = slice(h * RET_DV, (h + 1) * RET_DV)
        yn = yc * lax.rsqrt(var + EPS) * gn_ref[:, sl]
        g = g_ref[:, sl]
        y_ref[:, sl] = (g * jax.nn.sigmoid(g) * yn).astype(y_ref.dtype)


def _retention(rq, rk, rv, rg, ret_gn):
    B, L, _ = rq.shape
    tb = TB_RET
    log_g = jnp.log(1.0 - 2.0 ** (-5.0 - jnp.arange(RET_HEADS, dtype=F32)))
    i = jnp.arange(tb, dtype=F32)
    ci = jnp.arange(tb) // CHUNK
    visible = (ci[None, :] <= ci[:, None]).astype(F32)
    dm = jnp.exp(log_g[:, None, None] * jnp.abs(i[:, None] - i[None, :])) * visible[None]
    ones = jnp.ones((1, 1, RET_DK), F32)
    qd = jnp.exp(log_g[:, None] * (i + 1.0)[None, :])[:, :, None] * ones
    kd = jnp.exp(log_g[:, None] * (tb - 1.0 - i)[None, :])[:, :, None] * ones
    cd = jnp.exp(log_g * tb)[:, None, None] * jnp.ones((1, RET_DK, RET_DV), F32)

    tok = lambda width: pl.BlockSpec((None, tb, width), lambda b, l: (b, l, 0))
    full = lambda shape: pl.BlockSpec(shape, lambda b, l: (0,) * len(shape))
    return pl.pallas_call(
        _retention_kernel,
        grid=(B, L // tb),
        in_specs=[tok(RET_QK), tok(RET_QK), tok(RET_V), tok(RET_V), full((1, RET_V)),
                  full((RET_HEADS, tb, tb)), full((RET_HEADS, tb, RET_DK)), full((RET_HEADS, tb, RET_DK)),
                  full((RET_HEADS, RET_DK, RET_DV))],
        out_specs=tok(RET_V),
        out_shape=jax.ShapeDtypeStruct((B, L, RET_V), BF16),
        scratch_shapes=[pltpu.VMEM((RET_HEADS, RET_DK, RET_DV), F32)],
        compiler_params=_params(("parallel", "arbitrary")),
        name="retention",
    )(rq, rk, rv, rg, ret_gn.reshape(1, RET_V), dm, qd, kd, cd)


def _dsa_kernel(iqT_ref, iwT_ref, ik_ref, dqT_ref, dk_ref, dvT_ref, tri_ref, y_ref,
                keys_ref, rhs_ref, m_ref, l_ref, acc_ref, s_ref, p_ref, alpha_ref, smax_ref, *, topk):
    qb = pl.program_id(1)
    nkb = (qb * QB) // KB + 1
    n_trips = (nkb + 1) // 2
    n_pairs = DSA_HEADS // 2

    key_row = lax.broadcasted_iota(I32, (KB, QB), 0)
    q_chunk = (qb * QB + lax.broadcasted_iota(I32, (KB, QB), 1)) // CHUNK

    def score_block(kb):
        k0 = pl.multiple_of(kb * KB, KB)
        r = jnp.dot(ik_ref[pl.ds(k0, KB), :], iqT_ref[...], preferred_element_type=F32)
        acc = jnp.zeros((KB, QB), F32)
        for h in range(IDX_HEADS):
            acc = acc + iwT_ref[h:h + 1, :] * jnp.maximum(r[:, h * QB:(h + 1) * QB], 0.0)
        acc = acc + 0.0
        bits = pltpu.bitcast(acc, I32)
        key = jnp.where(bits < 0, bits ^ 0x7FFFFFFF, bits)
        admissible = (k0 + key_row) // CHUNK <= q_chunk
        keys_ref[kb] = jnp.where(admissible, key, INT_MIN)

    def score_two_blocks(t, carry):
        score_block(2 * t)
        score_block(2 * t + 1)
        return carry

    lax.fori_loop(0, n_trips, score_two_blocks, 0)

    def count_ge(cand):
        def body(t, cnt):
            for kb in (2 * t, 2 * t + 1):
                hit = jnp.where(keys_ref[kb] >= cand, 1.0, 0.0)
                cnt = cnt + jnp.sum(hit.reshape(KB // SM_ROWS, SM_ROWS, QB), axis=0)
            return cnt

        cnt = lax.fori_loop(0, n_trips, body, jnp.zeros((SM_ROWS, QB), F32))
        return jnp.sum(cnt, axis=0, keepdims=True).astype(I32)

    zero = jnp.zeros((1, QB), I32)
    t0 = jnp.where(count_ge(zero) >= topk, zero, zero + INT_MIN)

    def bit_step(n, t):
        cand = t + jnp.left_shift(jnp.int32(1), 30 - n)
        return jnp.where(count_ge(cand) >= topk, cand, t)

    thr = lax.fori_loop(0, 31, bit_step, t0)
    n_gt = count_ge(thr + 1)
    need = jnp.where(thr == INT_MIN, 0, topk - n_gt).astype(F32)

    d_row = lax.broadcasted_iota(I32, (LANES, QB), 0)
    eye = jnp.where(d_row == lax.broadcasted_iota(I32, (LANES, QB), 1), 1.0, 0.0).astype(BF16)
    for j in range(n_pairs):
        qT = dqT_ref[j * LANES:(j + 1) * LANES, :]
        rhs_ref[j, :LANES, :QB] = jnp.where(d_row < DSA_DH, qT, jnp.zeros_like(qT))
        rhs_ref[j, :LANES, QB:] = jnp.where(d_row >= DSA_DH, qT, jnp.zeros_like(qT))
        rhs_ref[j, LANES:, :QB] = eye
        rhs_ref[j, LANES:, QB:] = eye
    m_ref[...] = jnp.full(m_ref.shape, M_INIT, F32)
    l_ref[...] = jnp.zeros(l_ref.shape, F32)
    acc_ref[...] = jnp.zeros(acc_ref.shape, F32)

    def logits(blk, slot, n_eq_before):
        kb = jnp.minimum(blk, nkb - 1)
        k0 = pl.multiple_of(kb * KB, KB)
        key = jnp.where(blk < nkb, keys_ref[kb], INT_MIN)
        eq = key == thr
        eq_f = jnp.where(eq, 1.0, 0.0)
        prefix = jnp.dot(tri_ref[...], eq_f.astype(BF16), preferred_element_type=F32)
        keep_tie = (n_eq_before + prefix) < need
        bias = jnp.where(key > thr, 0.0, jnp.where(eq, jnp.where(keep_tie, 0.0, NEG_BIG), NEG_BIG)).astype(BF16)
        for j in range(n_pairs):
            lhs = jnp.concatenate([dk_ref[pl.ds(k0, KB), j * LANES:(j + 1) * LANES], bias], axis=1)
            s = jnp.dot(lhs, rhs_ref[j], preferred_element_type=F32)
            s_ref[slot, j] = s
            smax_ref[slot, j] = jnp.max(s, axis=0, keepdims=True)
        return n_eq_before + jnp.sum(eq_f, axis=0, keepdims=True)

    def softmax_update(slot):
        rows = [slice(r * SM_ROWS, (r + 1) * SM_ROWS) for r in range(KB // SM_ROWS)]
        for j in range(n_pairs):
            m_prev = m_ref[j]
            m_new = jnp.maximum(m_prev, smax_ref[slot, j])
            alpha = jnp.exp(m_prev - m_new)
            psum = jnp.zeros((SM_ROWS, 2 * QB), F32)
            for r in rows:
                p = jnp.exp(s_ref[slot, j, r, :] - m_new)
                psum = psum + p
                p_ref[slot, j, r, :] = p.astype(BF16)
            l_ref[j] = alpha * l_ref[j] + jnp.sum(psum, axis=0, keepdims=True)
            m_ref[j] = m_new
            alpha_ref[slot, j] = alpha

    def accumulate(kb, slot):
        for j in range(n_pairs):
            pv = jnp.dot(dvT_ref[kb, j * LANES:(j + 1) * LANES, :], p_ref[slot, j], preferred_element_type=F32)
            acc_ref[j] = alpha_ref[slot, j] * acc_ref[j] + pv

    p_ref[1] = jnp.zeros(p_ref.shape[1:], BF16)
    alpha_ref[1] = jnp.ones(alpha_ref.shape[1:], F32)
    n_eq0 = logits(0, 0, jnp.zeros((1, QB), F32))

    def attend_two_blocks(t, n_eq):
        b = 2 * t
        n_eq = logits(b + 1, 1, n_eq)
        softmax_update(0)
        accumulate(jnp.maximum(b - 1, 0), 1)
        n_eq = logits(b + 2, 0, n_eq)
        softmax_update(1)
        accumulate(b, 0)
        return n_eq

    lax.fori_loop(0, n_trips, attend_two_blocks, n_eq0)
    accumulate(jnp.minimum(2 * n_trips - 1, nkb - 1), 1)

    outs = []
    for j in range(n_pairs):
        acc, l = acc_ref[j], l_ref[j]
        outs.append(acc[:DSA_DH, :QB] / l[:, :QB])
        outs.append(acc[DSA_DH:, QB:] / l[:, QB:])
    y_ref[...] = jnp.concatenate(outs, axis=0).T.astype(y_ref.dtype)


def _dsa(iqT, iwT, ik, dqT, dk, dvT):
    B, L, _ = dk.shape
    topk = min(DSA_TOPK_MAX, L // 4)
    r = jnp.arange(KB)
    tri = (r[None, :] < r[:, None]).astype(BF16)
    per_batch = lambda shape: pl.BlockSpec((None,) + shape, lambda b, q: (b,) + (0,) * len(shape))
    return pl.pallas_call(
        functools.partial(_dsa_kernel, topk=topk),
        grid=(B, L // QB),
        in_specs=[pl.BlockSpec((None, None, IDX_DH, IDX_HEADS * QB), lambda b, q: (b, q, 0, 0)),
                  pl.BlockSpec((None, 8, QB), lambda b, q: (b, 0, q)),
                  per_batch((L, IDX_DH)),
                  pl.BlockSpec((None, DSA_W, QB), lambda b, q: (b, 0, q)),
                  per_batch((L, DSA_W)),
                  per_batch((L // KB, DSA_W, KB)),
                  pl.BlockSpec((KB, KB), lambda b, q: (0, 0))],
        out_specs=pl.BlockSpec((None, QB, DSA_W), lambda b, q: (b, q, 0)),
        out_shape=jax.ShapeDtypeStruct((B, L, DSA_W), BF16),
        scratch_shapes=[pltpu.VMEM((L // KB, KB, QB), I32),
                        pltpu.VMEM((DSA_HEADS // 2, 2 * LANES, 2 * QB), BF16),
                        pltpu.VMEM((DSA_HEADS // 2, 1, 2 * QB), F32),
                        pltpu.VMEM((DSA_HEADS // 2, 1, 2 * QB), F32),
                        pltpu.VMEM((DSA_HEADS // 2, LANES, 2 * QB), F32),
                        pltpu.VMEM((2, DSA_HEADS // 2, KB, 2 * QB), F32),
                        pltpu.VMEM((2, DSA_HEADS // 2, KB, 2 * QB), BF16),
                        pltpu.VMEM((2, DSA_HEADS // 2, 1, 2 * QB), F32),
                        pltpu.VMEM((2, DSA_HEADS // 2, 1, 2 * QB), F32)],
        compiler_params=_params(("parallel", "arbitrary")),
        name="dsa",
    )(iqT, iwT, ik, dqT, dk, dvT, tri)


def _merge_kernel(yr_ref, yd_ref, gate_ref, x_ref, wr_ref, wd_ref, wo_ref, fn_ref, x1_ref, h2T_ref, *, d_model):
    g = gate_ref[...]
    br = jnp.dot(yr_ref[...], wr_ref[...], preferred_element_type=F32)
    bd = jnp.dot(yd_ref[...], wd_ref[...], preferred_element_type=F32)
    merged = jax.nn.sigmoid(g[:, :d_model]) * br + jax.nn.sigmoid(g[:, d_model:]) * bd
    x1 = x_ref[...] + jnp.dot(merged.astype(BF16), wo_ref[...], preferred_element_type=F32)
    x1_ref[...] = x1
    ms = jnp.mean(x1 * x1, axis=-1, keepdims=True)
    h2 = x1 * lax.rsqrt(ms + EPS) * fn_ref[...]
    h2T_ref[...] = h2.T.astype(BF16)


def _merge(y_ret, y_dsa, gates, x, w_ret_o, w_dsa_o, w_out, ffn_norm):
    N, D = x.shape
    tm = TM_MERGE
    tok = lambda width: pl.BlockSpec((tm, width), lambda t: (t, 0))
    full = lambda shape: pl.BlockSpec(shape, lambda t: (0, 0))
    return pl.pallas_call(
        functools.partial(_merge_kernel, d_model=D),
        grid=(N // tm,),
        in_specs=[tok(RET_V), tok(DSA_W), tok(2 * D), tok(D),
                  full((RET_V, D)), full((DSA_W, D)), full((D, D)), full((1, D))],
        out_specs=(tok(D), pl.BlockSpec((D, tm), lambda t: (0, t))),
        out_shape=(jax.ShapeDtypeStruct((N, D), F32), jax.ShapeDtypeStruct((D, N), BF16)),
        compiler_params=_params(("parallel",)),
        name="merge",
    )(y_ret, y_dsa, gates, x, w_ret_o.astype(BF16), w_dsa_o.astype(BF16), w_out.astype(BF16),
      ffn_norm.reshape(1, D))


_ROW_LEN = [PEER_TOPK // (a + 1) for a in range(PEER_TOPK)]


def _extract_rounds(s, n, break_ties):
    rows = lax.broadcasted_iota(I32, s.shape, 0).astype(F32)
    rank = jnp.full(s.shape, float(n), F32)
    vals = []
    for a in range(n):
        m = jnp.max(s, axis=0, keepdims=True)
        hit = s == m
        if break_ties:
            first = jnp.min(jnp.where(hit, rows, float(s.shape[0])), axis=0, keepdims=True)
            hit = rows == first
        s = jnp.where(hit, -jnp.inf, s)
        rank = jnp.where(hit, float(a), rank)
        vals.append(m)
    return vals, rank


def _extract_top_pair(s_a, s_b, n):
    fast = (_extract_rounds(s_a, n, break_ties=False), _extract_rounds(s_b, n, break_ties=False))
    taken = sum(jnp.sum(jnp.where(rank < float(n), 1.0, 0.0), axis=0, keepdims=True) for _, rank in fast)
    return lax.cond(jnp.max(taken) > float(2 * n),
                    lambda: (_extract_rounds(s_a, n, break_ties=True), _extract_rounds(s_b, n, break_ties=True)),
                    lambda: fast)


def _peer_route_kernel(h2T_ref, wqT_ref, sk_ref, r1_ref, p1_ref, n_ref, p0_ref):
    h2T = h2T_ref[...]
    K = PEER_TOPK
    for h in range(PEER_HEADS):
        sT = []
        for c in range(2):
            r0 = (h * 2 + c) * PEER_NKEYS
            qT = jnp.dot(wqT_ref[r0:r0 + PEER_NKEYS, :], h2T, preferred_element_type=F32)
            sT.append(jnp.dot(sk_ref[h, c], qT.astype(BF16), preferred_element_type=F32))
        (v0, rank0), (v1, rank1) = _extract_top_pair(sT[0], sT[1], K)
        cand = [[v0[a] + v1[b] for b in range(_ROW_LEN[a])] for a in range(K)]
        best, _ = _extract_rounds(jnp.concatenate([c for row in cand for c in row], axis=0), K, break_ties=True)
        tau = best[K - 1]
        z = jnp.zeros_like(tau)
        for k in range(K):
            z = z + jnp.exp(best[k] - best[0])
        gt = [sum(jnp.where(c > tau, 1.0, 0.0) for c in row) for row in cand]
        eq = [sum(jnp.where(c == tau, 1.0, 0.0) for c in row) for row in cand]
        need = float(K) - sum(gt)
        n = jnp.zeros(rank0.shape, F32)
        for a in range(K):
            n_a = gt[a] + jnp.minimum(jnp.maximum(need, 0.0), eq[a])
            need = need - eq[a]
            n = jnp.where(rank0 == float(a), n_a, n)
        r1_ref[h] = rank1.astype(BF16)
        p1_ref[h] = jnp.exp(sT[1] - v1[0]).astype(BF16)
        n_ref[h] = n.astype(BF16)
        p0_ref[h] = (0.5 * jnp.exp(sT[0] - v0[0]) / z).astype(BF16)


def _peer_route(h2T, peer_wq, peer_subkeys):
    D, N = h2T.shape
    tm = TM_ROUTE
    wqT = peer_wq.T.astype(BF16)
    sk = peer_subkeys.astype(BF16)
    spec = pl.BlockSpec((PEER_HEADS, PEER_NKEYS, tm), lambda t: (0, 0, t))
    shape = lambda dt: jax.ShapeDtypeStruct((PEER_HEADS, PEER_NKEYS, N), dt)
    return pl.pallas_call(
        _peer_route_kernel,
        grid=(N // tm,),
        in_specs=[pl.BlockSpec((D, tm), lambda t: (0, t)),
                  pl.BlockSpec(wqT.shape, lambda t: (0, 0)),
                  pl.BlockSpec(sk.shape, lambda t: (0, 0, 0, 0))],
        out_specs=(spec, spec, spec, spec),
        out_shape=(shape(BF16), shape(BF16), shape(BF16), shape(BF16)),
        compiler_params=_params(("parallel",)),
        name="peer_route",
    )(h2T, wqT, sk)


def _peer_dense_kernel(h2T_ref, r1_ref, p1_ref, n_ref, p0_ref, u_ref, vT_ref, x1_ref, fn_ref, out_ref,
                       acc_ref, a_ref, g_ref, r1s_ref, p1s_ref, *, n_groups):
    step = pl.program_id(1)
    tm = h2T_ref.shape[1]

    def project(slot):
        a_ref[slot] = jnp.dot(u_ref[...], h2T_ref[...], preferred_element_type=F32)

    def gate(slot):
        for c in range(tm // LANES):
            sl = slice(c * LANES, (c + 1) * LANES)
            for il in range(IG_PEER):
                rows = slice(il * PEER_NKEYS, (il + 1) * PEER_NKEYS)
                w = jnp.zeros((PEER_NKEYS, LANES), BF16)
                for h in range(PEER_HEADS):
                    n_i = n_ref[h, il:il + 1, sl]
                    p0 = p0_ref[h, il:il + 1, sl]
                    w = w + jnp.where(r1s_ref[h, :, sl] < n_i, p1s_ref[h, :, sl] * p0, jnp.zeros_like(w))
                a = a_ref[slot, rows, sl]
                gelu2 = a * (1.0 + lax.erf(a * (2.0 ** -0.5)))
                g_ref[slot, rows, sl] = (w.astype(F32) * gelu2).astype(BF16)

    def combine(slot):
        acc_ref[...] += jnp.dot(vT_ref[...], g_ref[slot], preferred_element_type=F32)

    @pl.when(step == 0)
    def _():
        acc_ref[...] = jnp.zeros_like(acc_ref)
        g_ref[1] = jnp.zeros(g_ref.shape[1:], BF16)
        r1s_ref[...] = r1_ref[...]
        p1s_ref[...] = p1_ref[...]
        project(0)

    @pl.when(jnp.logical_and(step > 0, step <= n_groups))
    def _():
        project(step % 2)
        gate((step - 1) % 2)
        combine(step % 2)

    @pl.when(step == n_groups + 1)
    def _():
        combine(step % 2)
        y = x1_ref[...] + acc_ref[...].T
        ms = jnp.mean(y * y, axis=-1, keepdims=True)
        out_ref[...] = y * lax.rsqrt(ms + EPS) * fn_ref[...]


def _peer_dense(h2T, r1, p1, n, p0, peer_u, peer_v, x1, final_norm):
    D, N = h2T.shape
    tm = TM_PEER
    n_exp = peer_u.shape[0]
    u = peer_u.astype(BF16)
    vT = peer_v.T.astype(BF16)
    rows = IG_PEER * PEER_NKEYS
    ng = n_exp // rows
    rt = pl.BlockSpec((PEER_HEADS, PEER_NKEYS, tm), lambda t, s: (0, 0, t))
    grp = pl.BlockSpec((PEER_HEADS, None, IG_PEER, tm), lambda t, s: (0, jnp.clip(s - 1, 0, ng - 1), 0, t))
    n = n.reshape(PEER_HEADS, ng, IG_PEER, N)
    p0 = p0.reshape(PEER_HEADS, ng, IG_PEER, N)
    return pl.pallas_call(
        functools.partial(_peer_dense_kernel, n_groups=ng),
        grid=(N // tm, ng + 2),
        in_specs=[pl.BlockSpec((D, tm), lambda t, s: (0, t)), rt, rt, grp, grp,
                  pl.BlockSpec((rows, D), lambda t, s: (jnp.minimum(s, ng - 1), 0)),
                  pl.BlockSpec((D, rows), lambda t, s: (0, jnp.clip(s - 2, 0, ng - 1))),
                  pl.BlockSpec((tm, D), lambda t, s: (t, 0)),
                  pl.BlockSpec((1, D), lambda t, s: (0, 0))],
        out_specs=pl.BlockSpec((tm, D), lambda t, s: (t, 0)),
        out_shape=jax.ShapeDtypeStruct((N, D), F32),
        scratch_shapes=[pltpu.VMEM((D, tm), F32), pltpu.VMEM((2, rows, tm), F32), pltpu.VMEM((2, rows, tm), BF16),
                        pltpu.VMEM((PEER_HEADS, PEER_NKEYS, tm), BF16), pltpu.VMEM((PEER_HEADS, PEER_NKEYS, tm), BF16)],
        compiler_params=_params(("parallel", "arbitrary")),
        name="peer_dense",
    )(h2T, r1, p1, n, p0, u, vT, x1, final_norm.reshape(1, D))


def kernel(x, attn_norm, w_in, ret_gn, w_ret_o, w_dsa_o, w_out, ffn_norm, peer_wq, peer_subkeys, peer_u, peer_v,
           final_norm):
    B, L, D = x.shape
    assert attn_norm.shape[0] == 1, "single-layer block"
    assert L % TM_IN == 0 and (B * L) % TM_PEER == 0 and D % LANES == 0 and QB == LANES
    rq, rk, rv, rg, dqT, dk, dvT, iqT, ik, iwT, gates = _inproj(x, attn_norm[0], w_in[0])
    y_ret = _retention(rq, rk, rv, rg, ret_gn[0])
    y_dsa = _dsa(iqT, iwT, ik, dqT, dk, dvT)
    x1, h2T = _merge(y_ret.reshape(B * L, RET_V), y_dsa.reshape(B * L, DSA_W), gates.reshape(B * L, 2 * D),
                     x.reshape(B * L, D), w_ret_o[0], w_dsa_o[0], w_out[0], ffn_norm[0])
    r1, p1, n, p0 = _peer_route(h2T, peer_wq[0], peer_subkeys[0])
    out = _peer_dense(h2T, r1, p1, n, p0, peer_u[0], peer_v[0], x1, final_norm)
    return out.reshape(B, L, D)
```

```python
import functools

import jax
import jax.numpy as jnp
import numpy as np
from jax import lax
from jax.experimental import pallas as pl
from jax.experimental.pallas import tpu as pltpu

F32 = jnp.float32
BF16 = jnp.bfloat16
I32 = jnp.int32

CHUNK = 64
EPS = 1e-6
RET_HEADS, RET_DK, RET_DV, RET_THETA = 4, 64, 128, 10000.0
DSA_HEADS, DSA_DH, IDX_HEADS, IDX_DH = 8, 64, 4, 64
DSA_TOPK_MAX = 256
ROPE_THETA, ROPE_DIM = 500000.0, 16
PEER_HEADS, PEER_NKEYS, PEER_DKEY, PEER_TOPK = 8, 128, 256, 16

LANES = 128
VMEM_LIMIT = 56 * 1024 * 1024

RET_QK = RET_HEADS * RET_DK
RET_V = RET_HEADS * RET_DV
DSA_W = DSA_HEADS * DSA_DH
IDX_Q = IDX_HEADS * IDX_DH

TM_IN = 512
TB_RET = 256
QB = 128
KB = 256
SM_ROWS = 64
TM_MERGE = 512
TM_ROUTE = 256
TM_PEER = 512
IG_PEER = 16
MXU_PIECES = 4

NEG_BIG = -1e30
M_INIT = -1e20
INT_MIN = -(2 ** 31)

_OFF = {}
_o = 0
for _name, _w in (("rq", RET_QK), ("rk", RET_QK), ("rv", RET_V), ("rg", RET_V), ("dq", DSA_W), ("dk", DSA_W),
                  ("dv", DSA_W), ("iq", IDX_Q), ("ik", LANES), ("iw", LANES), ("gates", None)):
    _OFF[_name] = _o
    if _w is not None:
        _o += _w


def _params(sem):
    return pltpu.CompilerParams(dimension_semantics=sem, vmem_limit_bytes=VMEM_LIMIT)


def _rope_chunk(p, cos, sin, half):
    lane = lax.broadcasted_iota(I32, p.shape, 1) % 64
    partner = jnp.where(lane < half, pltpu.roll(p, LANES - half, 1), pltpu.roll(p, half, 1))
    return p * cos + partner * sin


def _inproj_kernel(x_ref, g_ref, w_ref, cr_ref, sr_ref, cp_ref, sp_ref,
                   rq_ref, rk_ref, rv_ref, rg_ref, dqT_ref, dk_ref, dvT_ref, iqT_ref, ik_ref, iwT_ref, gate_ref,
                   *, d_model):
    x = x_ref[...]
    ms = jnp.mean(x * x, axis=-1, keepdims=True)
    h = (x * lax.rsqrt(ms + EPS) * g_ref[...]).astype(BF16)

    def proj(off, width):
        return jnp.dot(h, w_ref[:, off:off + width], preferred_element_type=F32)

    cr, sr, cp, sp = cr_ref[...], sr_ref[...], cp_ref[...], sp_ref[...]
    qk_scale = RET_DK ** -0.5
    for c in range(RET_QK // LANES):
        sl = slice(c * LANES, (c + 1) * LANES)
        rq_ref[:, sl] = _rope_chunk(proj(_OFF["rq"] + c * LANES, LANES), cr, sr, RET_DK // 2) * qk_scale
        rk_ref[:, sl] = _rope_chunk(proj(_OFF["rk"] + c * LANES, LANES), cr, sr, RET_DK // 2)
    rv_ref[...] = proj(_OFF["rv"], RET_V).astype(BF16)
    rg_ref[...] = proj(_OFF["rg"], RET_V)
    tm = x.shape[0]
    dh_scale = DSA_DH ** -0.5
    for c in range(DSA_W // LANES):
        sl = slice(c * LANES, (c + 1) * LANES)
        dq = _rope_chunk(proj(_OFF["dq"] + c * LANES, LANES), cp, sp, ROPE_DIM // 2) * dh_scale
        dqT_ref[sl, :] = dq.T.astype(BF16)
        dk_ref[:, sl] = _rope_chunk(proj(_OFF["dk"] + c * LANES, LANES), cp, sp, ROPE_DIM // 2).astype(BF16)
        dv = proj(_OFF["dv"] + c * LANES, LANES)
        for j in range(tm // KB):
            dvT_ref[j, sl, :] = dv[j * KB:(j + 1) * KB, :].T.astype(BF16)
    for c in range(IDX_Q // LANES):
        iq = _rope_chunk(proj(_OFF["iq"] + c * LANES, LANES), cp, sp, ROPE_DIM // 2)
        for j in range(tm // QB):
            t = iq[j * QB:(j + 1) * QB, :].T.astype(BF16)
            iqT_ref[j, :, (2 * c) * QB:(2 * c + 1) * QB] = t[:IDX_DH, :]
            iqT_ref[j, :, (2 * c + 1) * QB:(2 * c + 2) * QB] = t[IDX_DH:, :]
    ik = _rope_chunk(proj(_OFF["ik"], LANES), cp, sp, ROPE_DIM // 2)
    ik_ref[...] = ik[:, :IDX_DH].astype(BF16)
    iw = proj(_OFF["iw"], LANES) * ((IDX_HEADS ** -0.5) * (IDX_DH ** -0.5))
    iwT_ref[...] = iw.T[:8, :]
    gate_ref[...] = proj(_OFF["gates"], 2 * d_model)


def _rope_tables(L, half, theta, pass_dims):
    inv = theta ** (-jnp.arange(half, dtype=F32) / half)
    ang = jnp.arange(L).astype(F32)[:, None] * inv[None, :]
    cos, sin = jnp.cos(ang), jnp.sin(ang)
    ones = jnp.ones((L, pass_dims), F32)
    zeros = jnp.zeros((L, pass_dims), F32)
    cos64 = jnp.concatenate([cos, cos, ones], axis=-1)
    sin64 = jnp.concatenate([-sin, sin, zeros], axis=-1)
    return jnp.tile(cos64, (1, 2)), jnp.tile(sin64, (1, 2))


def _inproj(x, attn_norm, w_in):
    B, L, D = x.shape
    tm = TM_IN
    nl = L // tm
    sizes = (RET_QK, RET_QK, RET_V, RET_V, DSA_W, DSA_W, DSA_W, IDX_Q, IDX_DH, IDX_HEADS, 2 * D)
    offs = np.concatenate([[0], np.cumsum(sizes)])
    cols = [w_in[:, offs[i]:offs[i + 1]] for i in range(len(sizes))]
    cols[8] = jnp.pad(cols[8], ((0, 0), (0, LANES - IDX_DH)))
    cols[9] = jnp.pad(cols[9], ((0, 0), (0, LANES - IDX_HEADS)))
    w = jnp.concatenate(cols, axis=1).astype(BF16)
    wtot = w.shape[1]
    cr, sr = _rope_tables(L, RET_DK // 2, RET_THETA, 0)
    cp, sp = _rope_tables(L, ROPE_DIM // 2, ROPE_THETA, DSA_DH - ROPE_DIM)

    tok = lambda width: pl.BlockSpec((None, tm, width), lambda b, l: (b, l, 0))
    tab = pl.BlockSpec((tm, LANES), lambda b, l: (l, 0))
    out_shape = (
        jax.ShapeDtypeStruct((B, L, RET_QK), F32),
        jax.ShapeDtypeStruct((B, L, RET_QK), F32),
        jax.ShapeDtypeStruct((B, L, RET_V), BF16),
        jax.ShapeDtypeStruct((B, L, RET_V), F32),
        jax.ShapeDtypeStruct((B, DSA_W, L), BF16),
        jax.ShapeDtypeStruct((B, L, DSA_W), BF16),
        jax.ShapeDtypeStruct((B, L // KB, DSA_W, KB), BF16),
        jax.ShapeDtypeStruct((B, L // QB, IDX_DH, IDX_HEADS * QB), BF16),
        jax.ShapeDtypeStruct((B, L, IDX_DH), BF16),
        jax.ShapeDtypeStruct((B, 8, L), F32),
        jax.ShapeDtypeStruct((B, L, 2 * D), F32),
    )
    out_specs = (
        tok(RET_QK), tok(RET_QK), tok(RET_V), tok(RET_V),
        pl.BlockSpec((None, DSA_W, tm), lambda b, l: (b, 0, l)),
        tok(DSA_W),
        pl.BlockSpec((None, tm // KB, DSA_W, KB), lambda b, l: (b, l, 0, 0)),
        pl.BlockSpec((None, tm // QB, IDX_DH, IDX_HEADS * QB), lambda b, l: (b, l, 0, 0)),
        tok(IDX_DH),
        pl.BlockSpec((None, 8, tm), lambda b, l: (b, 0, l)),
        tok(2 * D),
    )
    return pl.pallas_call(
        functools.partial(_inproj_kernel, d_model=D),
        grid=(B, nl),
        in_specs=[tok(D), pl.BlockSpec((1, D), lambda b, l: (0, 0)),
                  pl.BlockSpec((D, wtot), lambda b, l: (0, 0)), tab, tab, tab, tab],
        out_specs=out_specs,
        out_shape=out_shape,
        compiler_params=_params(("parallel", "parallel")),
        name="inproj",
    )(x, attn_norm.reshape(1, D), w, cr, sr, cp, sp)


def _retention_kernel(q_ref, k_ref, v_ref, g_ref, gn_ref, dm_ref, qd_ref, kd_ref, cd_ref, y_ref, state_ref):
    @pl.when(pl.program_id(1) == 0)
    def _():
        state_ref[...] = jnp.zeros_like(state_ref)

    for h in range(RET_HEADS):
        q = q_ref[:, h * RET_DK:(h + 1) * RET_DK]
        k = k_ref[:, h * RET_DK:(h + 1) * RET_DK]
        v = v_ref[:, h * RET_DV:(h + 1) * RET_DV]
        s = lax.dot_general(q.astype(BF16), k.astype(BF16), (((1,), (1,)), ((), ())), preferred_element_type=F32)
        intra = jnp.dot((s * dm_ref[h]).astype(BF16), v, preferred_element_type=F32)
        state = state_ref[h]
        cross = jnp.dot((q * qd_ref[h]).astype(BF16), state.astype(BF16), preferred_element_type=F32)
        kv = lax.dot_general((k * kd_ref[h]).astype(BF16), v, (((0,), (0,)), ((), ())), preferred_element_type=F32)
        state_ref[h] = cd_ref[h] * state + kv
        y = intra + cross
        mu = jnp.mean(y, axis=-1, keepdims=True)
        yc = y - mu
        var = jnp.mean(yc * yc, axis=-1, keepdims=True)
        sl = slice(h * RET_DV, (h + 1) * RET_DV)
        yn = yc * lax.rsqrt(var + EPS) * gn_ref[:, sl]
        g = g_ref[:, sl]
        y_ref[:, sl] = (g * jax.nn.sigmoid(g) * yn).astype(y_ref.dtype)


def _retention(rq, rk, rv, rg, ret_gn):
    B, L, _ = rq.shape
    tb = TB_RET
    log_g = jnp.log(1.0 - 2.0 ** (-5.0 - jnp.arange(RET_HEADS, dtype=F32)))
    i = jnp.arange(tb, dtype=F32)
    ci = jnp.arange(tb) // CHUNK
    visible = (ci[None, :] <= ci[:, None]).astype(F32)
    dm = jnp.exp(log_g[:, None, None] * jnp.abs(i[:, None] - i[None, :])) * visible[None]
    ones = jnp.ones((1, 1, RET_DK), F32)
    qd = jnp.exp(log_g[:, None] * (i + 1.0)[None, :])[:, :, None] * ones
    kd = jnp.exp(log_g[:, None] * (tb - 1.0 - i)[None, :])[:, :, None] * ones
    cd = jnp.exp(log_g * tb)[:, None, None] * jnp.ones((1, RET_DK, RET_DV), F32)

    tok = lambda width: pl.BlockSpec((None, tb, width), lambda b, l: (b, l, 0))
    full = lambda shape: pl.BlockSpec(shape, lambda b, l: (0,) * len(shape))
    return pl.pallas_call(
        _retention_kernel,
        grid=(B, L // tb),
        in_specs=[tok(RET_QK), tok(RET_QK), tok(RET_V), tok(RET_V), full((1, RET_V)),
                  full((RET_HEADS, tb, tb)), full((RET_HEADS, tb, RET_DK)), full((RET_HEADS, tb, RET_DK)),
                  full((RET_HEADS, RET_DK, RET_DV))],
        out_specs=tok(RET_V),
        out_shape=jax.ShapeDtypeStruct((B, L, RET_V), BF16),
        scratch_shapes=[pltpu.VMEM((RET_HEADS, RET_DK, RET_DV), F32)],
        compiler_params=_params(("parallel", "arbitrary")),
        name="retention",
    )(rq, rk, rv, rg, ret_gn.reshape(1, RET_V), dm, qd, kd, cd)


def _dsa_kernel(iqT_ref, iwT_ref, ik_ref, dqT_ref, dk_ref, dvT_ref, tri_ref, y_ref,
                keys_ref, rhs_ref, m_ref, l_ref, acc_ref, s_ref, p_ref, alpha_ref, smax_ref, *, topk):
    qb = pl.program_id(1)
    nkb = (qb * QB) // KB + 1
    n_trips = (nkb + 1) // 2
    n_pairs = DSA_HEADS // 2

    key_row = lax.broadcasted_iota(I32, (KB, QB), 0)
    q_chunk = (qb * QB + lax.broadcasted_iota(I32, (KB, QB), 1)) // CHUNK

    def score_block(kb):
        k0 = pl.multiple_of(kb * KB, KB)
        r = jnp.dot(ik_ref[pl.ds(k0, KB), :], iqT_ref[...], preferred_element_type=F32)
        acc = jnp.zeros((KB, QB), F32)
        for h in range(IDX_HEADS):
            acc = acc + iwT_ref[h:h + 1, :] * jnp.maximum(r[:, h * QB:(h + 1) * QB], 0.0)
        acc = acc + 0.0
        bits = pltpu.bitcast(acc, I32)
        key = jnp.where(bits < 0, bits ^ 0x7FFFFFFF, bits)
        admissible = (k0 + key_row) // CHUNK <= q_chunk
        keys_ref[kb] = jnp.where(admissible, key, INT_MIN)

    def score_two_blocks(t, carry):
        score_block(2 * t)
        score_block(2 * t + 1)
        return carry

    lax.fori_loop(0, n_trips, score_two_blocks, 0)

    def count_ge(cand):
        def body(t, cnt):
            for kb in (2 * t, 2 * t + 1):
                hit = jnp.where(keys_ref[kb] >= cand, 1.0, 0.0)
                cnt = cnt + jnp.sum(hit.reshape(KB // SM_ROWS, SM_ROWS, QB), axis=0)
            return cnt

        cnt = lax.fori_loop(0, n_trips, body, jnp.zeros((SM_ROWS, QB), F32))
        return jnp.sum(cnt, axis=0, keepdims=True).astype(I32)

    zero = jnp.zeros((1, QB), I32)
    t0 = jnp.where(count_ge(zero) >= topk, zero, zero + INT_MIN)

    def bit_step(n, t):
        cand = t + jnp.left_shift(jnp.int32(1), 30 - n)
        return jnp.where(count_ge(cand) >= topk, cand, t)

    thr = lax.fori_loop(0, 31, bit_step, t0)
    n_gt = count_ge(thr + 1)
    need = jnp.where(thr == INT_MIN, 0, topk - n_gt).astype(F32)

    d_row = lax.broadcasted_iota(I32, (LANES, QB), 0)
    eye = jnp.where(d_row == lax.broadcasted_iota(I32, (LANES, QB), 1), 1.0, 0.0).astype(BF16)
    for j in range(n_pairs):
        qT = dqT_ref[j * LANES:(j + 1) * LANES, :]
        rhs_ref[j, :LANES, :QB] = jnp.where(d_row < DSA_DH, qT, jnp.zeros_like(qT))
        rhs_ref[j, :LANES, QB:] = jnp.where(d_row >= DSA_DH, qT, jnp.zeros_like(qT))
        rhs_ref[j, LANES:, :QB] = eye
        rhs_ref[j, LANES:, QB:] = eye
    m_ref[...] = jnp.full(m_ref.shape, M_INIT, F32)
    l_ref[...] = jnp.zeros(l_ref.shape, F32)
    acc_ref[...] = jnp.zeros(acc_ref.shape, F32)

    def logits(blk, slot, n_eq_before):
        kb = jnp.minimum(blk, nkb - 1)
        k0 = pl.multiple_of(kb * KB, KB)
        key = jnp.where(blk < nkb, keys_ref[kb], INT_MIN)
        eq = key == thr
        eq_f = jnp.where(eq, 1.0, 0.0)
        prefix = jnp.dot(tri_ref[...], eq_f.astype(BF16), preferred_element_type=F32)
        keep_tie = (n_eq_before + prefix) < need
        bias = jnp.where(key > thr, 0.0, jnp.where(eq, jnp.where(keep_tie, 0.0, NEG_BIG), NEG_BIG)).astype(BF16)
        for j in range(n_pairs):
            lhs = jnp.concatenate([dk_ref[pl.ds(k0, KB), j * LANES:(j + 1) * LANES], bias], axis=1)
            s = jnp.dot(lhs, rhs_ref[j], preferred_element_type=F32)
            s_ref[slot, j] = s
            smax_ref[slot, j] = jnp.max(s, axis=0, keepdims=True)
        return n_eq_before + jnp.sum(eq_f, axis=0, keepdims=True)

    def softmax_update(slot):
        rows = [slice(r * SM_ROWS, (r + 1) * SM_ROWS) for r in range(KB // SM_ROWS)]
        for j in range(n_pairs):
            m_prev = m_ref[j]
            m_new = jnp.maximum(m_prev, smax_ref[slot, j])
            alpha = jnp.exp(m_prev - m_new)
            psum = jnp.zeros((SM_ROWS, 2 * QB), F32)
            for r in rows:
                p = jnp.exp(s_ref[slot, j, r, :] - m_new)
                psum = psum + p
                p_ref[slot, j, r, :] = p.astype(BF16)
            l_ref[j] = alpha * l_ref[j] + jnp.sum(psum, axis=0, keepdims=True)
            m_ref[j] = m_new
            alpha_ref[slot, j] = alpha

    def accumulate(kb, slot):
        for j in range(n_pairs):
            pv = jnp.dot(dvT_ref[kb, j * LANES:(j + 1) * LANES, :], p_ref[slot, j], preferred_element_type=F32)
            acc_ref[j] = alpha_ref[slot, j] * acc_ref[j] + pv

    p_ref[1] = jnp.zeros(p_ref.shape[1:], BF16)
    alpha_ref[1] = jnp.ones(alpha_ref.shape[1:], F32)
    n_eq0 = logits(0, 0, jnp.zeros((1, QB), F32))

    def attend_two_blocks(t, n_eq):
        b = 2 * t
        n_eq = logits(b + 1, 1, n_eq)
        softmax_update(0)
        accumulate(jnp.maximum(b - 1, 0), 1)
        n_eq = logits(b + 2, 0, n_eq)
        softmax_update(1)
        accumulate(b, 0)
        return n_eq

    lax.fori_loop(0, n_trips, attend_two_blocks, n_eq0)
    accumulate(jnp.minimum(2 * n_trips - 1, nkb - 1), 1)

    outs = []
    for j in range(n_pairs):
        acc, l = acc_ref[j], l_ref[j]
        outs.append(acc[:DSA_DH, :QB] / l[:, :QB])
        outs.append(acc[DSA_DH:, QB:] / l[:, QB:])
    y_ref[...] = jnp.concatenate(outs, axis=0).T.astype(y_ref.dtype)


def _dsa(iqT, iwT, ik, dqT, dk, dvT):
    B, L, _ = dk.shape
    topk = min(DSA_TOPK_MAX, L // 4)
    r = jnp.arange(KB)
    tri = (r[None, :] < r[:, None]).astype(BF16)
    per_batch = lambda shape: pl.BlockSpec((None,) + shape, lambda b, q: (b,) + (0,) * len(shape))
    return pl.pallas_call(
        functools.partial(_dsa_kernel, topk=topk),
        grid=(B, L // QB),
        in_specs=[pl.BlockSpec((None, None, IDX_DH, IDX_HEADS * QB), lambda b, q: (b, q, 0, 0)),
                  pl.BlockSpec((None, 8, QB), lambda b, q: (b, 0, q)),
                  per_batch((L, IDX_DH)),
                  pl.BlockSpec((None, DSA_W, QB), lambda b, q: (b, 0, q)),
                  per_batch((L, DSA_W)),
                  per_batch((L // KB, DSA_W, KB)),
                  pl.BlockSpec((KB, KB), lambda b, q: (0, 0))],
        out_specs=pl.BlockSpec((None, QB, DSA_W), lambda b, q: (b, q, 0)),
        out_shape=jax.ShapeDtypeStruct((B, L, DSA_W), BF16),
        scratch_shapes=[pltpu.VMEM((L // KB, KB, QB), I32),
                        pltpu.VMEM((DSA_HEADS // 2, 2 * LANES, 2 * QB), BF16),
                        pltpu.VMEM((DSA_HEADS // 2, 1, 2 * QB), F32),
                        pltpu.VMEM((DSA_HEADS // 2, 1, 2 * QB), F32),
                        pltpu.VMEM((DSA_HEADS // 2, LANES, 2 * QB), F32),
                        pltpu.VMEM((2, DSA_HEADS // 2, KB, 2 * QB), F32),
                        pltpu.VMEM((2, DSA_HEADS // 2, KB, 2 * QB), BF16),
                        pltpu.VMEM((2, DSA_HEADS // 2, 1, 2 * QB), F32),
                        pltpu.VMEM((2, DSA_HEADS // 2, 1, 2 * QB), F32)],
        compiler_params=_params(("parallel", "arbitrary")),
        name="dsa",
    )(iqT, iwT, ik, dqT, dk, dvT, tri)


def _merge_kernel(yr_ref, yd_ref, gate_ref, x_ref, wr_ref, wd_ref, wo_ref, fn_ref, x1_ref, h2T_ref, *, d_model):
    g = gate_ref[...]
    br = jnp.dot(yr_ref[...], wr_ref[...], preferred_element_type=F32)
    bd = jnp.dot(yd_ref[...], wd_ref[...], preferred_element_type=F32)
    merged = jax.nn.sigmoid(g[:, :d_model]) * br + jax.nn.sigmoid(g[:, d_model:]) * bd
    x1 = x_ref[...] + jnp.dot(merged.astype(BF16), wo_ref[...], preferred_element_type=F32)
    x1_ref[...] = x1
    ms = jnp.mean(x1 * x1, axis=-1, keepdims=True)
    h2 = x1 * lax.rsqrt(ms + EPS) * fn_ref[...]
    h2T_ref[...] = h2.T.astype(BF16)


def _merge(y_ret, y_dsa, gates, x, w_ret_o, w_dsa_o, w_out, ffn_norm):
    N, D = x.shape
    tm = TM_MERGE
    tok = lambda width: pl.BlockSpec((tm, width), lambda t: (t, 0))
    full = lambda shape: pl.BlockSpec(shape, lambda t: (0, 0))
    return pl.pallas_call(
        functools.partial(_merge_kernel, d_model=D),
        grid=(N // tm,),
        in_specs=[tok(RET_V), tok(DSA_W), tok(2 * D), tok(D),
                  full((RET_V, D)), full((DSA_W, D)), full((D, D)), full((1, D))],
        out_specs=(tok(D), pl.BlockSpec((D, tm), lambda t: (0, t))),
        out_shape=(jax.ShapeDtypeStruct((N, D), F32), jax.ShapeDtypeStruct((D, N), BF16)),
        compiler_params=_params(("parallel",)),
        name="merge",
    )(y_ret, y_dsa, gates, x, w_ret_o.astype(BF16), w_dsa_o.astype(BF16), w_out.astype(BF16),
      ffn_norm.reshape(1, D))


_ROW_LEN = [PEER_TOPK // (a + 1) for a in range(PEER_TOPK)]


def _extract_rounds(s, n, break_ties):
    rows = lax.broadcasted_iota(I32, s.shape, 0).astype(F32)
    rank = jnp.full(s.shape, float(n), F32)
    vals = []
    for a in range(n):
        m = jnp.max(s, axis=0, keepdims=True)
        hit = s == m
        if break_ties:
            first = jnp.min(jnp.where(hit, rows, float(s.shape[0])), axis=0, keepdims=True)
            hit = rows == first
        s = jnp.where(hit, -jnp.inf, s)
        rank = jnp.where(hit, float(a), rank)
        vals.append(m)
    return vals, rank


def _extract_top_pair(s_a, s_b, n):
    fast = (_extract_rounds(s_a, n, break_ties=False), _extract_rounds(s_b, n, break_ties=False))
    taken = sum(jnp.sum(jnp.where(rank < float(n), 1.0, 0.0), axis=0, keepdims=True) for _, rank in fast)
    return lax.cond(jnp.max(taken) > float(2 * n),
                    lambda: (_extract_rounds(s_a, n, break_ties=True), _extract_rounds(s_b, n, break_ties=True)),
                    lambda: fast)


def _peer_route_kernel(h2T_ref, wqT_ref, sk_ref, r1_ref, p1_ref, n_ref, p0_ref):
    h2T = h2T_ref[...]
    K = PEER_TOPK
    for h in range(PEER_HEADS):
        sT = []
        for c in range(2):
            r0 = (h * 2 + c) * PEER_NKEYS
            qT = jnp.dot(wqT_ref[r0:r0 + PEER_NKEYS, :], h2T, preferred_element_type=F32)
            sT.append(jnp.dot(sk_ref[h, c], qT.astype(BF16), preferred_element_type=F32))
        (v0, rank0), (v1, rank1) = _extract_top_pair(sT[0], sT[1], K)
        cand = [[v0[a] + v1[b] for b in range(_ROW_LEN[a])] for a in range(K)]
        best, _ = _extract_rounds(jnp.concatenate([c for row in cand for c in row], axis=0), K, break_ties=True)
        tau = best[K - 1]
        z = jnp.zeros_like(tau)
        for k in range(K):
            z = z + jnp.exp(best[k] - best[0])
        gt = [sum(jnp.where(c > tau, 1.0, 0.0) for c in row) for row in cand]
        eq = [sum(jnp.where(c == tau, 1.0, 0.0) for c in row) for row in cand]
        need = float(K) - sum(gt)
        n = jnp.zeros(rank0.shape, F32)
        for a in range(K):
            n_a = gt[a] + jnp.minimum(jnp.maximum(need, 0.0), eq[a])
            need = need - eq[a]
            n = jnp.where(rank0 == float(a), n_a, n)
        r1_ref[h] = rank1.astype(BF16)
        p1_ref[h] = jnp.exp(sT[1] - v1[0]).astype(BF16)
        n_ref[h] = n.astype(BF16)
        p0_ref[h] = (0.5 * jnp.exp(sT[0] - v0[0]) / z).astype(BF16)


def _peer_route(h2T, peer_wq, peer_subkeys):
    D, N = h2T.shape
    tm = TM_ROUTE
    wqT = peer_wq.T.astype(BF16)
    sk = peer_subkeys.astype(BF16)
    spec = pl.BlockSpec((PEER_HEADS, PEER_NKEYS, tm), lambda t: (0, 0, t))
    shape = lambda dt: jax.ShapeDtypeStruct((PEER_HEADS, PEER_NKEYS, N), dt)
    return pl.pallas_call(
        _peer_route_kernel,
        grid=(N // tm,),
        in_specs=[pl.BlockSpec((D, tm), lambda t: (0, t)),
                  pl.BlockSpec(wqT.shape, lambda t: (0, 0)),
                  pl.BlockSpec(sk.shape, lambda t: (0, 0, 0, 0))],
        out_specs=(spec, spec, spec, spec),
        out_shape=(shape(BF16), shape(BF16), shape(BF16), shape(BF16)),
        compiler_params=_params(("parallel",)),
        name="peer_route",
    )(h2T, wqT, sk)


def _peer_dense_kernel(h2T_ref, r1_ref, p1_ref, n_ref, p0_ref, u_ref, vT_ref, x1_ref, fn_ref, out_ref,
                       acc_ref, a_ref, g_ref, r1s_ref, p1s_ref, *, n_groups):
    step = pl.program_id(1)
    tm = h2T_ref.shape[1]

    def project(slot):
        piece = u_ref.shape[0] // MXU_PIECES
        for p in range(MXU_PIECES):
            rows = slice(p * piece, (p + 1) * piece)
            a_ref[slot, rows, :] = jnp.dot(u_ref[rows, :], h2T_ref[...], preferred_element_type=F32)

    def gate(slot):
        for c in range(tm // LANES):
            sl = slice(c * LANES, (c + 1) * LANES)
            for il in range(IG_PEER):
                rows = slice(il * PEER_NKEYS, (il + 1) * PEER_NKEYS)
                w = jnp.zeros((PEER_NKEYS, LANES), BF16)
                for h in range(PEER_HEADS):
                    n_i = n_ref[h, il:il + 1, sl]
                    p0 = p0_ref[h, il:il + 1, sl]
                    w = w + jnp.where(r1s_ref[h, :, sl] < n_i, p1s_ref[h, :, sl] * p0, jnp.zeros_like(w))
                a = a_ref[slot, rows, sl]
                gelu2 = a * (1.0 + lax.erf(a * (2.0 ** -0.5)))
                g_ref[slot, rows, sl] = (w.astype(F32) * gelu2).astype(BF16)

    def combine(slot):
        acc_ref[...] += jnp.dot(vT_ref[...], g_ref[slot], preferred_element_type=F32)

    @pl.when(step == 0)
    def _():
        acc_ref[...] = jnp.zeros_like(acc_ref)
        g_ref[1] = jnp.zeros(g_ref.shape[1:], BF16)
        r1s_ref[...] = r1_ref[...]
        p1s_ref[...] = p1_ref[...]
        project(0)

    @pl.when(jnp.logical_and(step > 0, step <= n_groups))
    def _():
        project(step % 2)
        gate((step - 1) % 2)
        combine(step % 2)

    @pl.when(step == n_groups + 1)
    def _():
        combine(step % 2)
        y = x1_ref[...] + acc_ref[...].T
        ms = jnp.mean(y * y, axis=-1, keepdims=True)
        out_ref[...] = y * lax.rsqrt(ms + EPS) * fn_ref[...]


def _peer_dense(h2T, r1, p1, n, p0, peer_u, peer_v, x1, final_norm):
    D, N = h2T.shape
    tm = TM_PEER
    n_exp = peer_u.shape[0]
    u = peer_u.astype(BF16)
    vT = peer_v.T.astype(BF16)
    rows = IG_PEER * PEER_NKEYS
    ng = n_exp // rows
    rt = pl.BlockSpec((PEER_HEADS, PEER_NKEYS, tm), lambda t, s: (0, 0, t))
    grp = pl.BlockSpec((PEER_HEADS, None, IG_PEER, tm), lambda t, s: (0, jnp.clip(s - 1, 0, ng - 1), 0, t))
    n = n.reshape(PEER_HEADS, ng, IG_PEER, N)
    p0 = p0.reshape(PEER_HEADS, ng, IG_PEER, N)
    return pl.pallas_call(
        functools.partial(_peer_dense_kernel, n_groups=ng),
        grid=(N // tm, ng + 2),
        in_specs=[pl.BlockSpec((D, tm), lambda t, s: (0, t)), rt, rt, grp, grp,
                  pl.BlockSpec((rows, D), lambda t, s: (jnp.minimum(s, ng - 1), 0)),
                  pl.BlockSpec((D, rows), lambda t, s: (0, jnp.clip(s - 2, 0, ng - 1))),
                  pl.BlockSpec((tm, D), lambda t, s: (t, 0)),
                  pl.BlockSpec((1, D), lambda t, s: (0, 0))],
        out_specs=pl.BlockSpec((tm, D), lambda t, s: (t, 0)),
        out_shape=jax.ShapeDtypeStruct((N, D), F32),
        scratch_shapes=[pltpu.VMEM((D, tm), F32), pltpu.VMEM((2, rows, tm), F32), pltpu.VMEM((2, rows, tm), BF16),
                        pltpu.VMEM((PEER_HEADS, PEER_NKEYS, tm), BF16), pltpu.VMEM((PEER_HEADS, PEER_NKEYS, tm), BF16)],
        compiler_params=_params(("parallel", "arbitrary")),
        name="peer_dense",
    )(h2T, r1, p1, n, p0, u, vT, x1, final_norm.reshape(1, D))


def kernel(x, attn_norm, w_in, ret_gn, w_ret_o, w_dsa_o, w_out, ffn_norm, peer_wq, peer_subkeys, peer_u, peer_v,
           final_norm):
    B, L, D = x.shape
    assert attn_norm.shape[0] == 1, "single-layer block"
    assert L % TM_IN == 0 and (B * L) % TM_PEER == 0 and D % LANES == 0 and QB == LANES
    rq, rk, rv, rg, dqT, dk, dvT, iqT, ik, iwT, gates = _inproj(x, attn_norm[0], w_in[0])
    y_ret = _retention(rq, rk, rv, rg, ret_gn[0])
    y_dsa = _dsa(iqT, iwT, ik, dqT, dk, dvT)
    x1, h2T = _merge(y_ret.reshape(B * L, RET_V), y_dsa.reshape(B * L, DSA_W), gates.reshape(B * L, 2 * D),
                     x.reshape(B * L, D), w_ret_o[0], w_dsa_o[0], w_out[0], ffn_norm[0])
    r1, p1, n, p0 = _peer_route(h2T, peer_wq[0], peer_subkeys[0])
    out = _peer_dense(h2T, r1, p1, n, p0, peer_u[0], peer_v[0], x1, final_norm)
    return out.reshape(B, L, D)
```

```python
import functools

import jax
import jax.numpy as jnp
import numpy as np
from jax import lax
from jax.experimental import pallas as pl
from jax.experimental.pallas import tpu as pltpu

F32 = jnp.float32
BF16 = jnp.bfloat16
I32 = jnp.int32

CHUNK = 64
EPS = 1e-6
RET_HEADS, RET_DK, RET_DV, RET_THETA = 4, 64, 128, 10000.0
DSA_HEADS, DSA_DH, IDX_HEADS, IDX_DH = 8, 64, 4, 64
DSA_TOPK_MAX = 256
ROPE_THETA, ROPE_DIM = 500000.0, 16
PEER_HEADS, PEER_NKEYS, PEER_TOPK = 8, 128, 16

LANES = 128
SUBLANES = 8
VMEM_LIMIT = 56 * 1024 * 1024

RET_QK = RET_HEADS * RET_DK
RET_V = RET_HEADS * RET_DV
DSA_W = DSA_HEADS * DSA_DH
IDX_Q = IDX_HEADS * IDX_DH

TM_IN = 512
TB_RET = 256
QB = 128
KB = 256
SM_ROWS = 64
TM_MERGE = 512
TM_ROUTE = 256
TM_PEER = 512
IG_PEER = 16
MXU_PIECES = 4

NEG_BIG = -1e30
M_INIT = -1e20
INT_MIN = -(2 ** 31)

_OFF = {}
_o = 0
for _name, _w in (("rq", RET_QK), ("rk", RET_QK), ("rv", RET_V), ("rg", RET_V), ("dq", DSA_W), ("dk", DSA_W),
                  ("dv", DSA_W), ("iq", IDX_Q), ("ik", LANES), ("iw", LANES), ("gates", None)):
    _OFF[_name] = _o
    if _w is not None:
        _o += _w


def _params(sem):
    return pltpu.CompilerParams(dimension_semantics=sem, vmem_limit_bytes=VMEM_LIMIT)


def _rope_chunk(p, cos, sin, half):
    lane = lax.broadcasted_iota(I32, p.shape, 1) % 64
    partner = jnp.where(lane < half, pltpu.roll(p, LANES - half, 1), pltpu.roll(p, half, 1))
    return p * cos + partner * sin


def _inproj_kernel(x_ref, g_ref, w_ref, cr_ref, sr_ref, cp_ref, sp_ref,
                   rq_ref, rk_ref, rv_ref, rg_ref, dqT_ref, dk_ref, dvT_ref, iqT_ref, ik_ref, iwT_ref, gate_ref,
                   *, d_model):
    x = x_ref[...]
    ms = jnp.mean(x * x, axis=-1, keepdims=True)
    h = (x * lax.rsqrt(ms + EPS) * g_ref[...]).astype(BF16)

    def proj(off, width):
        return jnp.dot(h, w_ref[:, off:off + width], preferred_element_type=F32)

    cr, sr, cp, sp = cr_ref[...], sr_ref[...], cp_ref[...], sp_ref[...]
    qk_scale = RET_DK ** -0.5
    for c in range(RET_QK // LANES):
        sl = slice(c * LANES, (c + 1) * LANES)
        rq_ref[:, sl] = _rope_chunk(proj(_OFF["rq"] + c * LANES, LANES), cr, sr, RET_DK // 2) * qk_scale
        rk_ref[:, sl] = _rope_chunk(proj(_OFF["rk"] + c * LANES, LANES), cr, sr, RET_DK // 2)
    rv_ref[...] = proj(_OFF["rv"], RET_V).astype(BF16)
    rg_ref[...] = proj(_OFF["rg"], RET_V)
    tm = x.shape[0]
    dh_scale = DSA_DH ** -0.5
    for c in range(DSA_W // LANES):
        sl = slice(c * LANES, (c + 1) * LANES)
        dq = _rope_chunk(proj(_OFF["dq"] + c * LANES, LANES), cp, sp, ROPE_DIM // 2) * dh_scale
        dqT_ref[sl, :] = dq.T.astype(BF16)
        dk_ref[:, sl] = _rope_chunk(proj(_OFF["dk"] + c * LANES, LANES), cp, sp, ROPE_DIM // 2).astype(BF16)
        dv = proj(_OFF["dv"] + c * LANES, LANES)
        for j in range(tm // KB):
            dvT_ref[j, sl, :] = dv[j * KB:(j + 1) * KB, :].T.astype(BF16)
    for c in range(IDX_Q // LANES):
        iq = _rope_chunk(proj(_OFF["iq"] + c * LANES, LANES), cp, sp, ROPE_DIM // 2)
        for j in range(tm // QB):
            t = iq[j * QB:(j + 1) * QB, :].T.astype(BF16)
            iqT_ref[j, :, (2 * c) * QB:(2 * c + 1) * QB] = t[:IDX_DH, :]
            iqT_ref[j, :, (2 * c + 1) * QB:(2 * c + 2) * QB] = t[IDX_DH:, :]
    ik = _rope_chunk(proj(_OFF["ik"], LANES), cp, sp, ROPE_DIM // 2)
    ik_ref[...] = ik[:, :IDX_DH].astype(BF16)
    iw = proj(_OFF["iw"], LANES) * ((IDX_HEADS ** -0.5) * (IDX_DH ** -0.5))
    iwT_ref[...] = iw.T[:SUBLANES, :]
    gate_ref[...] = proj(_OFF["gates"], 2 * d_model)


def _rope_tables(L, half, theta, pass_dims):
    inv = theta ** (-jnp.arange(half, dtype=F32) / half)
    ang = jnp.arange(L).astype(F32)[:, None] * inv[None, :]
    cos, sin = jnp.cos(ang), jnp.sin(ang)
    ones = jnp.ones((L, pass_dims), F32)
    zeros = jnp.zeros((L, pass_dims), F32)
    cos64 = jnp.concatenate([cos, cos, ones], axis=-1)
    sin64 = jnp.concatenate([-sin, sin, zeros], axis=-1)
    return jnp.tile(cos64, (1, 2)), jnp.tile(sin64, (1, 2))


def _inproj(x, attn_norm, w_in):
    B, L, D = x.shape
    tm = TM_IN
    nl = L // tm
    sizes = (RET_QK, RET_QK, RET_V, RET_V, DSA_W, DSA_W, DSA_W, IDX_Q, IDX_DH, IDX_HEADS, 2 * D)
    offs = np.concatenate([[0], np.cumsum(sizes)])
    cols = [w_in[:, offs[i]:offs[i + 1]] for i in range(len(sizes))]
    cols[8] = jnp.pad(cols[8], ((0, 0), (0, LANES - IDX_DH)))
    cols[9] = jnp.pad(cols[9], ((0, 0), (0, LANES - IDX_HEADS)))
    w = jnp.concatenate(cols, axis=1).astype(BF16)
    wtot = w.shape[1]
    cr, sr = _rope_tables(L, RET_DK // 2, RET_THETA, 0)
    cp, sp = _rope_tables(L, ROPE_DIM // 2, ROPE_THETA, DSA_DH - ROPE_DIM)

    tok = lambda width: pl.BlockSpec((None, tm, width), lambda b, l: (b, l, 0))
    tab = pl.BlockSpec((tm, LANES), lambda b, l: (l, 0))
    out_shape = (
        jax.ShapeDtypeStruct((B, L, RET_QK), F32),
        jax.ShapeDtypeStruct((B, L, RET_QK), F32),
        jax.ShapeDtypeStruct((B, L, RET_V), BF16),
        jax.ShapeDtypeStruct((B, L, RET_V), F32),
        jax.ShapeDtypeStruct((B, DSA_W, L), BF16),
        jax.ShapeDtypeStruct((B, L, DSA_W), BF16),
        jax.ShapeDtypeStruct((B, L // KB, DSA_W, KB), BF16),
        jax.ShapeDtypeStruct((B, L // QB, IDX_DH, IDX_HEADS * QB), BF16),
        jax.ShapeDtypeStruct((B, L, IDX_DH), BF16),
        jax.ShapeDtypeStruct((B, SUBLANES, L), F32),
        jax.ShapeDtypeStruct((B, L, 2 * D), F32),
    )
    out_specs = (
        tok(RET_QK), tok(RET_QK), tok(RET_V), tok(RET_V),
        pl.BlockSpec((None, DSA_W, tm), lambda b, l: (b, 0, l)),
        tok(DSA_W),
        pl.BlockSpec((None, tm // KB, DSA_W, KB), lambda b, l: (b, l, 0, 0)),
        pl.BlockSpec((None, tm // QB, IDX_DH, IDX_HEADS * QB), lambda b, l: (b, l, 0, 0)),
        tok(IDX_DH),
        pl.BlockSpec((None, SUBLANES, tm), lambda b, l: (b, 0, l)),
        tok(2 * D),
    )
    return pl.pallas_call(
        functools.partial(_inproj_kernel, d_model=D),
        grid=(B, nl),
        in_specs=[tok(D), pl.BlockSpec((1, D), lambda b, l: (0, 0)),
                  pl.BlockSpec((D, wtot), lambda b, l: (0, 0)), tab, tab, tab, tab],
        out_specs=out_specs,
        out_shape=out_shape,
        compiler_params=_params(("parallel", "parallel")),
        name="inproj",
    )(x, attn_norm.reshape(1, D), w, cr, sr, cp, sp)


def _retention_kernel(q_ref, k_ref, v_ref, g_ref, gn_ref, dm_ref, qd_ref, kd_ref, cd_ref, y_ref, state_ref):
    @pl.when(pl.program_id(1) == 0)
    def _():
        state_ref[...] = jnp.zeros_like(state_ref)

    for h in range(RET_HEADS):
        q = q_ref[:, h * RET_DK:(h + 1) * RET_DK]
        k = k_ref[:, h * RET_DK:(h + 1) * RET_DK]
        v = v_ref[:, h * RET_DV:(h + 1) * RET_DV]
        s = lax.dot_general(q.astype(BF16), k.astype(BF16), (((1,), (1,)), ((), ())), preferred_element_type=F32)
        intra = jnp.dot((s * dm_ref[h]).astype(BF16), v, preferred_element_type=F32)
        state = state_ref[h]
        cross = jnp.dot((q * qd_ref[h]).astype(BF16), state.astype(BF16), preferred_element_type=F32)
        kv = lax.dot_general((k * kd_ref[h]).astype(BF16), v, (((0,), (0,)), ((), ())), preferred_element_type=F32)
        state_ref[h] = cd_ref[h] * state + kv
        y = intra + cross
        mu = jnp.mean(y, axis=-1, keepdims=True)
        yc = y - mu
        var = jnp.mean(yc * yc, axis=-1, keepdims=True)
        sl = slice(h * RET_DV, (h + 1) * RET_DV)
        yn = yc * lax.rsqrt(var + EPS) * gn_ref[:, sl]
        g = g_ref[:, sl]
        y_ref[:, sl] = (g * jax.nn.sigmoid(g) * yn).astype(y_ref.dtype)


def _retention(rq, rk, rv, rg, ret_gn):
    B, L, _ = rq.shape
    tb = TB_RET
    log_g = jnp.log(1.0 - 2.0 ** (-5.0 - jnp.arange(RET_HEADS, dtype=F32)))
    i = jnp.arange(tb, dtype=F32)
    ci = jnp.arange(tb) // CHUNK
    visible = (ci[None, :] <= ci[:, None]).astype(F32)
    dm = jnp.exp(log_g[:, None, None] * jnp.abs(i[:, None] - i[None, :])) * visible[None]
    ones = jnp.ones((1, 1, RET_DK), F32)
    qd = jnp.exp(log_g[:, None] * (i + 1.0)[None, :])[:, :, None] * ones
    kd = jnp.exp(log_g[:, None] * (tb - 1.0 - i)[None, :])[:, :, None] * ones
    cd = jnp.exp(log_g * tb)[:, None, None] * jnp.ones((1, RET_DK, RET_DV), F32)

    tok = lambda width: pl.BlockSpec((None, tb, width), lambda b, l: (b, l, 0))
    full = lambda shape: pl.BlockSpec(shape, lambda b, l: (0,) * len(shape))
    return pl.pallas_call(
        _retention_kernel,
        grid=(B, L // tb),
        in_specs=[tok(RET_QK), tok(RET_QK), tok(RET_V), tok(RET_V), full((1, RET_V)),
                  full((RET_HEADS, tb, tb)), full((RET_HEADS, tb, RET_DK)), full((RET_HEADS, tb, RET_DK)),
                  full((RET_HEADS, RET_DK, RET_DV))],
        out_specs=tok(RET_V),
        out_shape=jax.ShapeDtypeStruct((B, L, RET_V), BF16),
        scratch_shapes=[pltpu.VMEM((RET_HEADS, RET_DK, RET_DV), F32)],
        compiler_params=_params(("parallel", "arbitrary")),
        name="retention",
    )(rq, rk, rv, rg, ret_gn.reshape(1, RET_V), dm, qd, kd, cd)


def _dsa_kernel(iqT_ref, iwT_ref, ik_ref, dqT_ref, dk_ref, dvT_ref, tri_ref, y_ref,
                keys_ref, rhs_ref, m_ref, l_ref, acc_ref, s_ref, p_ref, alpha_ref, smax_ref, *, topk):
    qb = pl.program_id(1)
    nkb = (qb * QB) // KB + 1
    n_trips = (nkb + 1) // 2
    n_pairs = DSA_HEADS // 2

    key_row = lax.broadcasted_iota(I32, (KB, QB), 0)
    q_chunk = (qb * QB + lax.broadcasted_iota(I32, (KB, QB), 1)) // CHUNK

    def score_block(kb):
        k0 = pl.multiple_of(kb * KB, KB)
        r = jnp.dot(ik_ref[pl.ds(k0, KB), :], iqT_ref[...], preferred_element_type=F32)
        acc = jnp.zeros((KB, QB), F32)
        for h in range(IDX_HEADS):
            acc = acc + iwT_ref[h:h + 1, :] * jnp.maximum(r[:, h * QB:(h + 1) * QB], 0.0)
        acc = acc + 0.0
        bits = pltpu.bitcast(acc, I32)
        key = jnp.where(bits < 0, bits ^ 0x7FFFFFFF, bits)
        admissible = (k0 + key_row) // CHUNK <= q_chunk
        keys_ref[kb] = jnp.where(admissible, key, INT_MIN)

    def score_two_blocks(t, carry):
        score_block(2 * t)
        score_block(2 * t + 1)
        return carry

    lax.fori_loop(0, n_trips, score_two_blocks, 0)

    def count_ge(cand):
        def body(t, cnt):
            for kb in (2 * t, 2 * t + 1):
                hit = jnp.where(keys_ref[kb] >= cand, 1.0, 0.0)
                cnt = cnt + jnp.sum(hit.reshape(KB // SM_ROWS, SM_ROWS, QB), axis=0)
            return cnt

        cnt = lax.fori_loop(0, n_trips, body, jnp.zeros((SM_ROWS, QB), F32))
        return jnp.sum(cnt, axis=0, keepdims=True).astype(I32)

    zero = jnp.zeros((1, QB), I32)
    t0 = jnp.where(count_ge(zero) >= topk, zero, zero + INT_MIN)

    def bit_step(n, t):
        cand = t + jnp.left_shift(jnp.int32(1), 30 - n)
        return jnp.where(count_ge(cand) >= topk, cand, t)

    thr = lax.fori_loop(0, 31, bit_step, t0)
    n_gt = count_ge(thr + 1)
    need = jnp.where(thr == INT_MIN, 0, topk - n_gt).astype(F32)

    d_row = lax.broadcasted_iota(I32, (LANES, QB), 0)
    eye = jnp.where(d_row == lax.broadcasted_iota(I32, (LANES, QB), 1), 1.0, 0.0).astype(BF16)
    for j in range(n_pairs):
        qT = dqT_ref[j * LANES:(j + 1) * LANES, :]
        rhs_ref[j, :LANES, :QB] = jnp.where(d_row < DSA_DH, qT, jnp.zeros_like(qT))
        rhs_ref[j, :LANES, QB:] = jnp.where(d_row >= DSA_DH, qT, jnp.zeros_like(qT))
        rhs_ref[j, LANES:, :QB] = eye
        rhs_ref[j, LANES:, QB:] = eye
    m_ref[...] = jnp.full(m_ref.shape, M_INIT, F32)
    l_ref[...] = jnp.zeros(l_ref.shape, F32)
    acc_ref[...] = jnp.zeros(acc_ref.shape, F32)

    def logits(blk, slot, n_eq_before):
        kb = jnp.minimum(blk, nkb - 1)
        k0 = pl.multiple_of(kb * KB, KB)
        key = jnp.where(blk < nkb, keys_ref[kb], INT_MIN)
        eq = key == thr
        eq_f = jnp.where(eq, 1.0, 0.0)
        prefix = jnp.dot(tri_ref[...], eq_f.astype(BF16), preferred_element_type=F32)
        keep_tie = (n_eq_before + prefix) < need
        bias = jnp.where(key > thr, 0.0, jnp.where(eq, jnp.where(keep_tie, 0.0, NEG_BIG), NEG_BIG)).astype(BF16)
        for j in range(n_pairs):
            lhs = jnp.concatenate([dk_ref[pl.ds(k0, KB), j * LANES:(j + 1) * LANES], bias], axis=1)
            s = jnp.dot(lhs, rhs_ref[j], preferred_element_type=F32)
            s_ref[slot, j] = s
            smax_ref[slot, j] = jnp.max(s, axis=0, keepdims=True)
        return n_eq_before + jnp.sum(eq_f, axis=0, keepdims=True)

    def softmax_update(slot):
        rows = [slice(r * SM_ROWS, (r + 1) * SM_ROWS) for r in range(KB // SM_ROWS)]
        for j in range(n_pairs):
            m_prev = m_ref[j]
            m_new = jnp.maximum(m_prev, smax_ref[slot, j])
            alpha = jnp.exp(m_prev - m_new)
            psum = jnp.zeros((SM_ROWS, 2 * QB), F32)
            for r in rows:
                p = jnp.exp(s_ref[slot, j, r, :] - m_new)
                psum = psum + p
                p_ref[slot, j, r, :] = p.astype(BF16)
            l_ref[j] = alpha * l_ref[j] + jnp.sum(psum, axis=0, keepdims=True)
            m_ref[j] = m_new
            alpha_ref[slot, j] = alpha

    def accumulate(kb, slot):
        for j in range(n_pairs):
            pv = jnp.dot(dvT_ref[kb, j * LANES:(j + 1) * LANES, :], p_ref[slot, j], preferred_element_type=F32)
            acc_ref[j] = alpha_ref[slot, j] * acc_ref[j] + pv

    p_ref[1] = jnp.zeros(p_ref.shape[1:], BF16)
    alpha_ref[1] = jnp.ones(alpha_ref.shape[1:], F32)
    n_eq0 = logits(0, 0, jnp.zeros((1, QB), F32))

    def attend_two_blocks(t, n_eq):
        b = 2 * t
        n_eq = logits(b + 1, 1, n_eq)
        softmax_update(0)
        accumulate(jnp.maximum(b - 1, 0), 1)
        n_eq = logits(b + 2, 0, n_eq)
        softmax_update(1)
        accumulate(b, 0)
        return n_eq

    lax.fori_loop(0, n_trips, attend_two_blocks, n_eq0)
    accumulate(jnp.minimum(2 * n_trips - 1, nkb - 1), 1)

    outs = []
    for j in range(n_pairs):
        acc, l = acc_ref[j], l_ref[j]
        outs.append(acc[:DSA_DH, :QB] / l[:, :QB])
        outs.append(acc[DSA_DH:, QB:] / l[:, QB:])
    y_ref[...] = jnp.concatenate(outs, axis=0).T.astype(y_ref.dtype)


def _dsa(iqT, iwT, ik, dqT, dk, dvT):
    B, L, _ = dk.shape
    topk = min(DSA_TOPK_MAX, L // 4)
    r = jnp.arange(KB)
    tri = (r[None, :] < r[:, None]).astype(BF16)
    per_batch = lambda shape: pl.BlockSpec((None,) + shape, lambda b, q: (b,) + (0,) * len(shape))
    return pl.pallas_call(
        functools.partial(_dsa_kernel, topk=topk),
        grid=(B, L // QB),
        in_specs=[pl.BlockSpec((None, None, IDX_DH, IDX_HEADS * QB), lambda b, q: (b, q, 0, 0)),
                  pl.BlockSpec((None, SUBLANES, QB), lambda b, q: (b, 0, q)),
                  per_batch((L, IDX_DH)),
                  pl.BlockSpec((None, DSA_W, QB), lambda b, q: (b, 0, q)),
                  per_batch((L, DSA_W)),
                  per_batch((L // KB, DSA_W, KB)),
                  pl.BlockSpec((KB, KB), lambda b, q: (0, 0))],
        out_specs=pl.BlockSpec((None, QB, DSA_W), lambda b, q: (b, q, 0)),
        out_shape=jax.ShapeDtypeStruct((B, L, DSA_W), BF16),
        scratch_shapes=[pltpu.VMEM((L // KB, KB, QB), I32),
                        pltpu.VMEM((DSA_HEADS // 2, 2 * LANES, 2 * QB), BF16),
                        pltpu.VMEM((DSA_HEADS // 2, 1, 2 * QB), F32),
                        pltpu.VMEM((DSA_HEADS // 2, 1, 2 * QB), F32),
                        pltpu.VMEM((DSA_HEADS // 2, LANES, 2 * QB), F32),
                        pltpu.VMEM((2, DSA_HEADS // 2, KB, 2 * QB), F32),
                        pltpu.VMEM((2, DSA_HEADS // 2, KB, 2 * QB), BF16),
                        pltpu.VMEM((2, DSA_HEADS // 2, 1, 2 * QB), F32),
                        pltpu.VMEM((2, DSA_HEADS // 2, 1, 2 * QB), F32)],
        compiler_params=_params(("parallel", "arbitrary")),
        name="dsa",
    )(iqT, iwT, ik, dqT, dk, dvT, tri)


def _merge_kernel(yr_ref, yd_ref, gate_ref, x_ref, wr_ref, wd_ref, wo_ref, fn_ref, x1_ref, h2T_ref, *, d_model):
    g = gate_ref[...]
    br = jnp.dot(yr_ref[...], wr_ref[...], preferred_element_type=F32)
    bd = jnp.dot(yd_ref[...], wd_ref[...], preferred_element_type=F32)
    merged = jax.nn.sigmoid(g[:, :d_model]) * br + jax.nn.sigmoid(g[:, d_model:]) * bd
    x1 = x_ref[...] + jnp.dot(merged.astype(BF16), wo_ref[...], preferred_element_type=F32)
    x1_ref[...] = x1
    ms = jnp.mean(x1 * x1, axis=-1, keepdims=True)
    h2 = x1 * lax.rsqrt(ms + EPS) * fn_ref[...]
    h2T_ref[...] = h2.T.astype(BF16)


def _merge(y_ret, y_dsa, gates, x, w_ret_o, w_dsa_o, w_out, ffn_norm):
    N, D = x.shape
    tm = TM_MERGE
    tok = lambda width: pl.BlockSpec((tm, width), lambda t: (t, 0))
    full = lambda shape: pl.BlockSpec(shape, lambda t: (0, 0))
    return pl.pallas_call(
        functools.partial(_merge_kernel, d_model=D),
        grid=(N // tm,),
        in_specs=[tok(RET_V), tok(DSA_W), tok(2 * D), tok(D),
                  full((RET_V, D)), full((DSA_W, D)), full((D, D)), full((1, D))],
        out_specs=(tok(D), pl.BlockSpec((D, tm), lambda t: (0, t))),
        out_shape=(jax.ShapeDtypeStruct((N, D), F32), jax.ShapeDtypeStruct((D, N), BF16)),
        compiler_params=_params(("parallel",)),
        name="merge",
    )(y_ret, y_dsa, gates, x, w_ret_o.astype(BF16), w_dsa_o.astype(BF16), w_out.astype(BF16),
      ffn_norm.reshape(1, D))


_ROW_LEN = [PEER_TOPK // (a + 1) for a in range(PEER_TOPK)]


def _extract_rounds(s, n, break_ties):
    rows = lax.broadcasted_iota(I32, s.shape, 0).astype(F32)
    rank = jnp.full(s.shape, float(n), F32)
    vals = []
    for a in range(n):
        m = jnp.max(s, axis=0, keepdims=True)
        hit = s == m
        if break_ties:
            first = jnp.min(jnp.where(hit, rows, float(s.shape[0])), axis=0, keepdims=True)
            hit = rows == first
        s = jnp.where(hit, -jnp.inf, s)
        rank = jnp.where(hit, float(a), rank)
        vals.append(m)
    return vals, rank


def _extract_top_pair(s_a, s_b, n):
    fast = (_extract_rounds(s_a, n, break_ties=False), _extract_rounds(s_b, n, break_ties=False))
    taken = sum(jnp.sum(jnp.where(rank < float(n), 1.0, 0.0), axis=0, keepdims=True) for _, rank in fast)
    return lax.cond(jnp.max(taken) > float(2 * n),
                    lambda: (_extract_rounds(s_a, n, break_ties=True), _extract_rounds(s_b, n, break_ties=True)),
                    lambda: fast)


def _peer_route_kernel(h2T_ref, wqT_ref, sk_ref, r1_ref, p1_ref, n_ref, p0_ref):
    h2T = h2T_ref[...]
    K = PEER_TOPK
    for h in range(PEER_HEADS):
        sT = []
        for c in range(2):
            r0 = (h * 2 + c) * PEER_NKEYS
            qT = jnp.dot(wqT_ref[r0:r0 + PEER_NKEYS, :], h2T, preferred_element_type=F32)
            sT.append(jnp.dot(sk_ref[h, c], qT.astype(BF16), preferred_element_type=F32))
        (v0, rank0), (v1, rank1) = _extract_top_pair(sT[0], sT[1], K)
        cand = [[v0[a] + v1[b] for b in range(_ROW_LEN[a])] for a in range(K)]
        best, _ = _extract_rounds(jnp.concatenate([c for row in cand for c in row], axis=0), K, break_ties=True)
        tau = best[K - 1]
        z = jnp.zeros_like(tau)
        for k in range(K):
            z = z + jnp.exp(best[k] - best[0])
        gt = [sum(jnp.where(c > tau, 1.0, 0.0) for c in row) for row in cand]
        eq = [sum(jnp.where(c == tau, 1.0, 0.0) for c in row) for row in cand]
        need = float(K) - sum(gt)
        n = jnp.zeros(rank0.shape, F32)
        for a in range(K):
            n_a = gt[a] + jnp.minimum(jnp.maximum(need, 0.0), eq[a])
            need = need - eq[a]
            n = jnp.where(rank0 == float(a), n_a, n)
        r1_ref[h] = rank1.astype(BF16)
        p1_ref[h] = jnp.exp(sT[1] - v1[0]).astype(BF16)
        n_ref[h] = n.astype(BF16)
        p0_ref[h] = (0.5 * jnp.exp(sT[0] - v0[0]) / z).astype(BF16)


def _peer_route(h2T, peer_wq, peer_subkeys):
    D, N = h2T.shape
    tm = TM_ROUTE
    wqT = peer_wq.T.astype(BF16)
    sk = peer_subkeys.astype(BF16)
    spec = pl.BlockSpec((PEER_HEADS, PEER_NKEYS, tm), lambda t: (0, 0, t))
    shape = lambda dt: jax.ShapeDtypeStruct((PEER_HEADS, PEER_NKEYS, N), dt)
    return pl.pallas_call(
        _peer_route_kernel,
        grid=(N // tm,),
        in_specs=[pl.BlockSpec((D, tm), lambda t: (0, t)),
                  pl.BlockSpec(wqT.shape, lambda t: (0, 0)),
                  pl.BlockSpec(sk.shape, lambda t: (0, 0, 0, 0))],
        out_specs=(spec, spec, spec, spec),
        out_shape=(shape(BF16), shape(BF16), shape(BF16), shape(BF16)),
        compiler_params=_params(("parallel",)),
        name="peer_route",
    )(h2T, wqT, sk)


def _peer_dense_kernel(h2T_ref, r1_ref, p1_ref, n_ref, p0_ref, u_ref, vT_ref, x1_ref, fn_ref, out_ref,
                       acc_ref, a_ref, g_ref, r1s_ref, p1s_ref, *, n_groups):
    step = pl.program_id(1)
    tm = h2T_ref.shape[1]

    def project(slot):
        piece = u_ref.shape[0] // MXU_PIECES
        for p in range(MXU_PIECES):
            rows = slice(p * piece, (p + 1) * piece)
            a_ref[slot, rows, :] = jnp.dot(u_ref[rows, :], h2T_ref[...], preferred_element_type=F32)

    def gate(slot):
        for c in range(tm // LANES):
            sl = slice(c * LANES, (c + 1) * LANES)
            for il in range(IG_PEER):
                rows = slice(il * PEER_NKEYS, (il + 1) * PEER_NKEYS)
                w = jnp.zeros((PEER_NKEYS, LANES), BF16)
                for h in range(PEER_HEADS):
                    n_i = n_ref[h, il:il + 1, sl]
                    p0 = p0_ref[h, il:il + 1, sl]
                    w = w + jnp.where(r1s_ref[h, :, sl] < n_i, p1s_ref[h, :, sl] * p0, jnp.zeros_like(w))
                a = a_ref[slot, rows, sl]
                gelu2 = a * (1.0 + lax.erf(a * (2.0 ** -0.5)))
                g_ref[slot, rows, sl] = (w.astype(F32) * gelu2).astype(BF16)

    def combine(slot):
        acc_ref[...] += jnp.dot(vT_ref[...], g_ref[slot], preferred_element_type=F32)

    @pl.when(step == 0)
    def _():
        acc_ref[...] = jnp.zeros_like(acc_ref)
        g_ref[1] = jnp.zeros(g_ref.shape[1:], BF16)
        r1s_ref[...] = r1_ref[...]
        p1s_ref[...] = p1_ref[...]
        project(0)

    @pl.when(jnp.logical_and(step > 0, step <= n_groups))
    def _():
        project(step % 2)
        gate((step - 1) % 2)
        combine(step % 2)

    @pl.when(step == n_groups + 1)
    def _():
        combine(step % 2)
        y = x1_ref[...] + acc_ref[...].T
        ms = jnp.mean(y * y, axis=-1, keepdims=True)
        out_ref[...] = y * lax.rsqrt(ms + EPS) * fn_ref[...]


def _peer_dense(h2T, r1, p1, n, p0, peer_u, peer_v, x1, final_norm):
    D, N = h2T.shape
    tm = TM_PEER
    n_exp = peer_u.shape[0]
    u = peer_u.astype(BF16)
    vT = peer_v.T.astype(BF16)
    rows = IG_PEER * PEER_NKEYS
    ng = n_exp // rows
    rt = pl.BlockSpec((PEER_HEADS, PEER_NKEYS, tm), lambda t, s: (0, 0, t))
    grp = pl.BlockSpec((PEER_HEADS, None, IG_PEER, tm), lambda t, s: (0, jnp.clip(s - 1, 0, ng - 1), 0, t))
    n = n.reshape(PEER_HEADS, ng, IG_PEER, N)
    p0 = p0.reshape(PEER_HEADS, ng, IG_PEER, N)
    return pl.pallas_call(
        functools.partial(_peer_dense_kernel, n_groups=ng),
        grid=(N // tm, ng + 2),
        in_specs=[pl.BlockSpec((D, tm), lambda t, s: (0, t)), rt, rt, grp, grp,
                  pl.BlockSpec((rows, D), lambda t, s: (jnp.minimum(s, ng - 1), 0)),
                  pl.BlockSpec((D, rows), lambda t, s: (0, jnp.clip(s - 2, 0, ng - 1))),
                  pl.BlockSpec((tm, D), lambda t, s: (t, 0)),
                  pl.BlockSpec((1, D), lambda t, s: (0, 0))],
        out_specs=pl.BlockSpec((tm, D), lambda t, s: (t, 0)),
        out_shape=jax.ShapeDtypeStruct((N, D), F32),
        scratch_shapes=[pltpu.VMEM((D, tm), F32), pltpu.VMEM((2, rows, tm), F32), pltpu.VMEM((2, rows, tm), BF16),
                        pltpu.VMEM((PEER_HEADS, PEER_NKEYS, tm), BF16), pltpu.VMEM((PEER_HEADS, PEER_NKEYS, tm), BF16)],
        compiler_params=_params(("parallel", "arbitrary")),
        name="peer_dense",
    )(h2T, r1, p1, n, p0, u, vT, x1, final_norm.reshape(1, D))


def kernel(x, attn_norm, w_in, ret_gn, w_ret_o, w_dsa_o, w_out, ffn_norm, peer_wq, peer_subkeys, peer_u, peer_v,
           final_norm):
    B, L, D = x.shape
    assert attn_norm.shape[0] == 1, "single-layer block"
    assert L % TM_IN == 0 and (B * L) % TM_PEER == 0 and D % LANES == 0 and QB == LANES
    rq, rk, rv, rg, dqT, dk, dvT, iqT, ik, iwT, gates = _inproj(x, attn_norm[0], w_in[0])
    y_ret = _retention(rq, rk, rv, rg, ret_gn[0])
    y_dsa = _dsa(iqT, iwT, ik, dqT, dk, dvT)
    x1, h2T = _merge(y_ret.reshape(B * L, RET_V), y_dsa.reshape(B * L, DSA_W), gates.reshape(B * L, 2 * D),
                     x.reshape(B * L, D), w_ret_o[0], w_dsa_o[0], w_out[0], ffn_norm[0])
    r1, p1, n, p0 = _peer_route(h2T, peer_wq[0], peer_subkeys[0])
    out = _peer_dense(h2T, r1, p1, n, p0, peer_u[0], peer_v[0], x1, final_norm)
    return out.reshape(B, L, D)
```

```python
import functools

import jax
import jax.numpy as jnp
import numpy as np
from jax import lax
from jax.experimental import pallas as pl
from jax.experimental.pallas import tpu as pltpu

F32 = jnp.float32
BF16 = jnp.bfloat16
I32 = jnp.int32

CHUNK = 64
EPS = 1e-6
RET_HEADS, RET_DK, RET_DV, RET_THETA = 4, 64, 128, 10000.0
DSA_HEADS, DSA_DH, IDX_HEADS, IDX_DH = 8, 64, 4, 64
DSA_TOPK_MAX = 256
ROPE_THETA, ROPE_DIM = 500000.0, 16
PEER_HEADS, PEER_NKEYS, PEER_TOPK = 8, 128, 16

LANES = 128
SUBLANES = 8
VMEM_LIMIT = 56 * 1024 * 1024

RET_QK = RET_HEADS * RET_DK
RET_V = RET_HEADS * RET_DV
DSA_W = DSA_HEADS * DSA_DH
IDX_Q = IDX_HEADS * IDX_DH

TM_IN = 512
TB_RET = 256
QB = 128
KB = 256
SM_ROWS = 64
TM_MERGE = 512
TM_ROUTE = 256
TM_PEER = 512
IG_PEER = 16
MXU_PIECES = 4

NEG_BIG = -1e30
M_INIT = -1e20
INT_MIN = -(2 ** 31)

_OFF = {}
_o = 0
for _name, _w in (("rq", RET_QK), ("rk", RET_QK), ("rv", RET_V), ("rg", RET_V), ("dq", DSA_W), ("dk", DSA_W),
                  ("dv", DSA_W), ("iq", IDX_Q), ("ik", LANES), ("iw", LANES), ("gates", None)):
    _OFF[_name] = _o
    if _w is not None:
        _o += _w


def _params(sem):
    return pltpu.CompilerParams(dimension_semantics=sem, vmem_limit_bytes=VMEM_LIMIT)


def _rope_chunk(p, cos, sin, half):
    lane = lax.broadcasted_iota(I32, p.shape, 1) % 64
    partner = jnp.where(lane < half, pltpu.roll(p, LANES - half, 1), pltpu.roll(p, half, 1))
    return p * cos + partner * sin


def _inproj_kernel(x_ref, g_ref, w_ref, cr_ref, sr_ref, cp_ref, sp_ref,
                   rq_ref, rk_ref, rv_ref, rg_ref, dqT_ref, dk_ref, dvT_ref, iqT_ref, ik_ref, iwT_ref, gate_ref,
                   *, d_model):
    x = x_ref[...]
    ms = jnp.mean(x * x, axis=-1, keepdims=True)
    h = (x * lax.rsqrt(ms + EPS) * g_ref[...]).astype(BF16)

    def proj(off, width):
        return jnp.dot(h, w_ref[:, off:off + width], preferred_element_type=F32)

    cr, sr, cp, sp = cr_ref[...], sr_ref[...], cp_ref[...], sp_ref[...]
    chunk = lambda p, c: p[:, c * LANES:(c + 1) * LANES]
    qk_scale = RET_DK ** -0.5
    p_rqk = proj(_OFF["rq"], 2 * RET_QK)
    for c in range(RET_QK // LANES):
        sl = slice(c * LANES, (c + 1) * LANES)
        rq_ref[:, sl] = _rope_chunk(chunk(p_rqk, c), cr, sr, RET_DK // 2) * qk_scale
        rk_ref[:, sl] = _rope_chunk(chunk(p_rqk, RET_QK // LANES + c), cr, sr, RET_DK // 2)
    rv_ref[...] = proj(_OFF["rv"], RET_V).astype(BF16)
    rg_ref[...] = proj(_OFF["rg"], RET_V)
    tm = x.shape[0]
    dh_scale = DSA_DH ** -0.5
    p_dq, p_dk, p_dv = proj(_OFF["dq"], DSA_W), proj(_OFF["dk"], DSA_W), proj(_OFF["dv"], DSA_W)
    for c in range(DSA_W // LANES):
        sl = slice(c * LANES, (c + 1) * LANES)
        dq = _rope_chunk(chunk(p_dq, c), cp, sp, ROPE_DIM // 2) * dh_scale
        dqT_ref[sl, :] = dq.T.astype(BF16)
        dk_ref[:, sl] = _rope_chunk(chunk(p_dk, c), cp, sp, ROPE_DIM // 2).astype(BF16)
        dv = chunk(p_dv, c)
        for j in range(tm // KB):
            dvT_ref[j, sl, :] = dv[j * KB:(j + 1) * KB, :].T.astype(BF16)
    p_iq = proj(_OFF["iq"], IDX_Q)
    for c in range(IDX_Q // LANES):
        iq = _rope_chunk(chunk(p_iq, c), cp, sp, ROPE_DIM // 2)
        for j in range(tm // QB):
            t = iq[j * QB:(j + 1) * QB, :].T.astype(BF16)
            iqT_ref[j, :, (2 * c) * QB:(2 * c + 1) * QB] = t[:IDX_DH, :]
            iqT_ref[j, :, (2 * c + 1) * QB:(2 * c + 2) * QB] = t[IDX_DH:, :]
    p_ikw = proj(_OFF["ik"], 2 * LANES)
    ik = _rope_chunk(chunk(p_ikw, 0), cp, sp, ROPE_DIM // 2)
    ik_ref[...] = ik[:, :IDX_DH].astype(BF16)
    iw = chunk(p_ikw, 1) * ((IDX_HEADS ** -0.5) * (IDX_DH ** -0.5))
    iwT_ref[...] = iw.T[:SUBLANES, :]
    gate_ref[...] = proj(_OFF["gates"], 2 * d_model)


def _rope_tables(L, half, theta, pass_dims):
    inv = theta ** (-jnp.arange(half, dtype=F32) / half)
    ang = jnp.arange(L).astype(F32)[:, None] * inv[None, :]
    cos, sin = jnp.cos(ang), jnp.sin(ang)
    ones = jnp.ones((L, pass_dims), F32)
    zeros = jnp.zeros((L, pass_dims), F32)
    cos64 = jnp.concatenate([cos, cos, ones], axis=-1)
    sin64 = jnp.concatenate([-sin, sin, zeros], axis=-1)
    return jnp.tile(cos64, (1, 2)), jnp.tile(sin64, (1, 2))


def _inproj(x, attn_norm, w_in):
    B, L, D = x.shape
    tm = TM_IN
    nl = L // tm
    sizes = (RET_QK, RET_QK, RET_V, RET_V, DSA_W, DSA_W, DSA_W, IDX_Q, IDX_DH, IDX_HEADS, 2 * D)
    offs = np.concatenate([[0], np.cumsum(sizes)])
    cols = [w_in[:, offs[i]:offs[i + 1]] for i in range(len(sizes))]
    cols[8] = jnp.pad(cols[8], ((0, 0), (0, LANES - IDX_DH)))
    cols[9] = jnp.pad(cols[9], ((0, 0), (0, LANES - IDX_HEADS)))
    w = jnp.concatenate(cols, axis=1).astype(BF16)
    wtot = w.shape[1]
    cr, sr = _rope_tables(L, RET_DK // 2, RET_THETA, 0)
    cp, sp = _rope_tables(L, ROPE_DIM // 2, ROPE_THETA, DSA_DH - ROPE_DIM)

    tok = lambda width: pl.BlockSpec((None, tm, width), lambda b, l: (b, l, 0))
    tab = pl.BlockSpec((tm, LANES), lambda b, l: (l, 0))
    out_shape = (
        jax.ShapeDtypeStruct((B, L, RET_QK), F32),
        jax.ShapeDtypeStruct((B, L, RET_QK), F32),
        jax.ShapeDtypeStruct((B, L, RET_V), BF16),
        jax.ShapeDtypeStruct((B, L, RET_V), F32),
        jax.ShapeDtypeStruct((B, DSA_W, L), BF16),
        jax.ShapeDtypeStruct((B, L, DSA_W), BF16),
        jax.ShapeDtypeStruct((B, L // KB, DSA_W, KB), BF16),
        jax.ShapeDtypeStruct((B, L // QB, IDX_DH, IDX_HEADS * QB), BF16),
        jax.ShapeDtypeStruct((B, L, IDX_DH), BF16),
        jax.ShapeDtypeStruct((B, SUBLANES, L), F32),
        jax.ShapeDtypeStruct((B, L, 2 * D), F32),
    )
    out_specs = (
        tok(RET_QK), tok(RET_QK), tok(RET_V), tok(RET_V),
        pl.BlockSpec((None, DSA_W, tm), lambda b, l: (b, 0, l)),
        tok(DSA_W),
        pl.BlockSpec((None, tm // KB, DSA_W, KB), lambda b, l: (b, l, 0, 0)),
        pl.BlockSpec((None, tm // QB, IDX_DH, IDX_HEADS * QB), lambda b, l: (b, l, 0, 0)),
        tok(IDX_DH),
        pl.BlockSpec((None, SUBLANES, tm), lambda b, l: (b, 0, l)),
        tok(2 * D),
    )
    return pl.pallas_call(
        functools.partial(_inproj_kernel, d_model=D),
        grid=(B, nl),
        in_specs=[tok(D), pl.BlockSpec((1, D), lambda b, l: (0, 0)),
                  pl.BlockSpec((D, wtot), lambda b, l: (0, 0)), tab, tab, tab, tab],
        out_specs=out_specs,
        out_shape=out_shape,
        compiler_params=_params(("parallel", "parallel")),
        name="inproj",
    )(x, attn_norm.reshape(1, D), w, cr, sr, cp, sp)


def _retention_kernel(q_ref, k_ref, v_ref, g_ref, gn_ref, dm_ref, qd_ref, kd_ref, cd_ref, y_ref, state_ref):
    @pl.when(pl.program_id(1) == 0)
    def _():
        state_ref[...] = jnp.zeros_like(state_ref)

    for h in range(RET_HEADS):
        q = q_ref[:, h * RET_DK:(h + 1) * RET_DK]
        k = k_ref[:, h * RET_DK:(h + 1) * RET_DK]
        v = v_ref[:, h * RET_DV:(h + 1) * RET_DV]
        s = lax.dot_general(q.astype(BF16), k.astype(BF16), (((1,), (1,)), ((), ())), preferred_element_type=F32)
        intra = jnp.dot((s * dm_ref[h]).astype(BF16), v, preferred_element_type=F32)
        state = state_ref[h]
        cross = jnp.dot((q * qd_ref[h]).astype(BF16), state.astype(BF16), preferred_element_type=F32)
        kv = lax.dot_general((k * kd_ref[h]).astype(BF16), v, (((0,), (0,)), ((), ())), preferred_element_type=F32)
        state_ref[h] = cd_ref[h] * state + kv
        y = intra + cross
        mu = jnp.mean(y, axis=-1, keepdims=True)
        yc = y - mu
        var = jnp.mean(yc * yc, axis=-1, keepdims=True)
        sl = slice(h * RET_DV, (h + 1) * RET_DV)
        yn = yc * lax.rsqrt(var + EPS) * gn_ref[:, sl]
        g = g_ref[:, sl]
        y_ref[:, sl] = (g * jax.nn.sigmoid(g) * yn).astype(y_ref.dtype)


def _retention(rq, rk, rv, rg, ret_gn):
    B, L, _ = rq.shape
    tb = TB_RET
    log_g = jnp.log(1.0 - 2.0 ** (-5.0 - jnp.arange(RET_HEADS, dtype=F32)))
    i = jnp.arange(tb, dtype=F32)
    ci = jnp.arange(tb) // CHUNK
    visible = (ci[None, :] <= ci[:, None]).astype(F32)
    dm = jnp.exp(log_g[:, None, None] * jnp.abs(i[:, None] - i[None, :])) * visible[None]
    ones = jnp.ones((1, 1, RET_DK), F32)
    qd = jnp.exp(log_g[:, None] * (i + 1.0)[None, :])[:, :, None] * ones
    kd = jnp.exp(log_g[:, None] * (tb - 1.0 - i)[None, :])[:, :, None] * ones
    cd = jnp.exp(log_g * tb)[:, None, None] * jnp.ones((1, RET_DK, RET_DV), F32)

    tok = lambda width: pl.BlockSpec((None, tb, width), lambda b, l: (b, l, 0))
    full = lambda shape: pl.BlockSpec(shape, lambda b, l: (0,) * len(shape))
    return pl.pallas_call(
        _retention_kernel,
        grid=(B, L // tb),
        in_specs=[tok(RET_QK), tok(RET_QK), tok(RET_V), tok(RET_V), full((1, RET_V)),
                  full((RET_HEADS, tb, tb)), full((RET_HEADS, tb, RET_DK)), full((RET_HEADS, tb, RET_DK)),
                  full((RET_HEADS, RET_DK, RET_DV))],
        out_specs=tok(RET_V),
        out_shape=jax.ShapeDtypeStruct((B, L, RET_V), BF16),
        scratch_shapes=[pltpu.VMEM((RET_HEADS, RET_DK, RET_DV), F32)],
        compiler_params=_params(("parallel", "arbitrary")),
        name="retention",
    )(rq, rk, rv, rg, ret_gn.reshape(1, RET_V), dm, qd, kd, cd)


def _dsa_kernel(iqT_ref, iwT_ref, ik_ref, dqT_ref, dk_ref, dvT_ref, tri_ref, y_ref,
                keys_ref, rhs_ref, m_ref, l_ref, acc_ref, s_ref, p_ref, alpha_ref, smax_ref, *, topk):
    qb = pl.program_id(1)
    nkb = (qb * QB) // KB + 1
    n_trips = (nkb + 1) // 2
    n_pairs = DSA_HEADS // 2

    key_row = lax.broadcasted_iota(I32, (KB, QB), 0)
    q_chunk = (qb * QB + lax.broadcasted_iota(I32, (KB, QB), 1)) // CHUNK

    def score_block(kb):
        k0 = pl.multiple_of(kb * KB, KB)
        r = jnp.dot(ik_ref[pl.ds(k0, KB), :], iqT_ref[...], preferred_element_type=F32)
        acc = jnp.zeros((KB, QB), F32)
        for h in range(IDX_HEADS):
            acc = acc + iwT_ref[h:h + 1, :] * jnp.maximum(r[:, h * QB:(h + 1) * QB], 0.0)
        acc = acc + 0.0
        bits = pltpu.bitcast(acc, I32)
        key = jnp.where(bits < 0, bits ^ 0x7FFFFFFF, bits)
        admissible = (k0 + key_row) // CHUNK <= q_chunk
        keys_ref[kb] = jnp.where(admissible, key, INT_MIN)

    def score_two_blocks(t, carry):
        score_block(2 * t)
        score_block(2 * t + 1)
        return carry

    lax.fori_loop(0, n_trips, score_two_blocks, 0)

    def count_ge(cand):
        def body(t, cnt):
            for kb in (2 * t, 2 * t + 1):
                hit = jnp.where(keys_ref[kb] >= cand, 1.0, 0.0)
                cnt = cnt + jnp.sum(hit.reshape(KB // SM_ROWS, SM_ROWS, QB), axis=0)
            return cnt

        cnt = lax.fori_loop(0, n_trips, body, jnp.zeros((SM_ROWS, QB), F32))
        return jnp.sum(cnt, axis=0, keepdims=True).astype(I32)

    zero = jnp.zeros((1, QB), I32)
    t0 = jnp.where(count_ge(zero) >= topk, zero, zero + INT_MIN)

    def bit_step(n, t):
        cand = t + jnp.left_shift(jnp.int32(1), 30 - n)
        return jnp.where(count_ge(cand) >= topk, cand, t)

    thr = lax.fori_loop(0, 31, bit_step, t0)
    n_gt = count_ge(thr + 1)
    need = jnp.where(thr == INT_MIN, 0, topk - n_gt).astype(F32)

    d_row = lax.broadcasted_iota(I32, (LANES, QB), 0)
    eye = jnp.where(d_row == lax.broadcasted_iota(I32, (LANES, QB), 1), 1.0, 0.0).astype(BF16)
    for j in range(n_pairs):
        qT = dqT_ref[j * LANES:(j + 1) * LANES, :]
        rhs_ref[j, :LANES, :QB] = jnp.where(d_row < DSA_DH, qT, jnp.zeros_like(qT))
        rhs_ref[j, :LANES, QB:] = jnp.where(d_row >= DSA_DH, qT, jnp.zeros_like(qT))
        rhs_ref[j, LANES:, :QB] = eye
        rhs_ref[j, LANES:, QB:] = eye
    m_ref[...] = jnp.full(m_ref.shape, M_INIT, F32)
    l_ref[...] = jnp.zeros(l_ref.shape, F32)
    acc_ref[...] = jnp.zeros(acc_ref.shape, F32)

    def logits(blk, slot, n_eq_before):
        kb = jnp.minimum(blk, nkb - 1)
        k0 = pl.multiple_of(kb * KB, KB)
        key = jnp.where(blk < nkb, keys_ref[kb], INT_MIN)
        eq = key == thr
        eq_f = jnp.where(eq, 1.0, 0.0)
        prefix = jnp.dot(tri_ref[...], eq_f.astype(BF16), preferred_element_type=F32)
        keep_tie = (n_eq_before + prefix) < need
        bias = jnp.where(key > thr, 0.0, jnp.where(eq, jnp.where(keep_tie, 0.0, NEG_BIG), NEG_BIG)).astype(BF16)
        for j in range(n_pairs):
            lhs = jnp.concatenate([dk_ref[pl.ds(k0, KB), j * LANES:(j + 1) * LANES], bias], axis=1)
            s = jnp.dot(lhs, rhs_ref[j], preferred_element_type=F32)
            s_ref[slot, j] = s
            smax_ref[slot, j] = jnp.max(s, axis=0, keepdims=True)
        return n_eq_before + jnp.sum(eq_f, axis=0, keepdims=True)

    def softmax_update(slot):
        rows = [slice(r * SM_ROWS, (r + 1) * SM_ROWS) for r in range(KB // SM_ROWS)]
        for j in range(n_pairs):
            m_prev = m_ref[j]
            m_new = jnp.maximum(m_prev, smax_ref[slot, j])
            alpha = jnp.exp(m_prev - m_new)
            psum = jnp.zeros((SM_ROWS, 2 * QB), F32)
            for r in rows:
                p = jnp.exp(s_ref[slot, j, r, :] - m_new)
                psum = psum + p
                p_ref[slot, j, r, :] = p.astype(BF16)
            l_ref[j] = alpha * l_ref[j] + jnp.sum(psum, axis=0, keepdims=True)
            m_ref[j] = m_new
            alpha_ref[slot, j] = alpha

    def accumulate(kb, slot):
        for j in range(n_pairs):
            pv = jnp.dot(dvT_ref[kb, j * LANES:(j + 1) * LANES, :], p_ref[slot, j], preferred_element_type=F32)
            acc_ref[j] = alpha_ref[slot, j] * acc_ref[j] + pv

    p_ref[1] = jnp.zeros(p_ref.shape[1:], BF16)
    alpha_ref[1] = jnp.ones(alpha_ref.shape[1:], F32)
    n_eq0 = logits(0, 0, jnp.zeros((1, QB), F32))

    def attend_two_blocks(t, n_eq):
        b = 2 * t
        n_eq = logits(b + 1, 1, n_eq)
        softmax_update(0)
        accumulate(jnp.maximum(b - 1, 0), 1)
        n_eq = logits(b + 2, 0, n_eq)
        softmax_update(1)
        accumulate(b, 0)
        return n_eq

    lax.fori_loop(0, n_trips, attend_two_blocks, n_eq0)
    accumulate(jnp.minimum(2 * n_trips - 1, nkb - 1), 1)

    outs = []
    for j in range(n_pairs):
        acc, l = acc_ref[j], l_ref[j]
        outs.append(acc[:DSA_DH, :QB] / l[:, :QB])
        outs.append(acc[DSA_DH:, QB:] / l[:, QB:])
    y_ref[...] = jnp.concatenate(outs, axis=0).T.astype(y_ref.dtype)


def _dsa(iqT, iwT, ik, dqT, dk, dvT):
    B, L, _ = dk.shape
    topk = min(DSA_TOPK_MAX, L // 4)
    r = jnp.arange(KB)
    tri = (r[None, :] < r[:, None]).astype(BF16)
    per_batch = lambda shape: pl.BlockSpec((None,) + shape, lambda b, q: (b,) + (0,) * len(shape))
    return pl.pallas_call(
        functools.partial(_dsa_kernel, topk=topk),
        grid=(B, L // QB),
        in_specs=[pl.BlockSpec((None, None, IDX_DH, IDX_HEADS * QB), lambda b, q: (b, q, 0, 0)),
                  pl.BlockSpec((None, SUBLANES, QB), lambda b, q: (b, 0, q)),
                  per_batch((L, IDX_DH)),
                  pl.BlockSpec((None, DSA_W, QB), lambda b, q: (b, 0, q)),
                  per_batch((L, DSA_W)),
                  per_batch((L // KB, DSA_W, KB)),
                  pl.BlockSpec((KB, KB), lambda b, q: (0, 0))],
        out_specs=pl.BlockSpec((None, QB, DSA_W), lambda b, q: (b, q, 0)),
        out_shape=jax.ShapeDtypeStruct((B, L, DSA_W), BF16),
        scratch_shapes=[pltpu.VMEM((L // KB, KB, QB), I32),
                        pltpu.VMEM((DSA_HEADS // 2, 2 * LANES, 2 * QB), BF16),
                        pltpu.VMEM((DSA_HEADS // 2, 1, 2 * QB), F32),
                        pltpu.VMEM((DSA_HEADS // 2, 1, 2 * QB), F32),
                        pltpu.VMEM((DSA_HEADS // 2, LANES, 2 * QB), F32),
                        pltpu.VMEM((2, DSA_HEADS // 2, KB, 2 * QB), F32),
                        pltpu.VMEM((2, DSA_HEADS // 2, KB, 2 * QB), BF16),
                        pltpu.VMEM((2, DSA_HEADS // 2, 1, 2 * QB), F32),
                        pltpu.VMEM((2, DSA_HEADS // 2, 1, 2 * QB), F32)],
        compiler_params=_params(("parallel", "arbitrary")),
        name="dsa",
    )(iqT, iwT, ik, dqT, dk, dvT, tri)


def _merge_kernel(yr_ref, yd_ref, gate_ref, x_ref, wr_ref, wd_ref, wo_ref, fn_ref, x1_ref, h2T_ref, *, d_model):
    g = gate_ref[...]
    br = jnp.dot(yr_ref[...], wr_ref[...], preferred_element_type=F32)
    bd = jnp.dot(yd_ref[...], wd_ref[...], preferred_element_type=F32)
    merged = jax.nn.sigmoid(g[:, :d_model]) * br + jax.nn.sigmoid(g[:, d_model:]) * bd
    x1 = x_ref[...] + jnp.dot(merged.astype(BF16), wo_ref[...], preferred_element_type=F32)
    x1_ref[...] = x1
    ms = jnp.mean(x1 * x1, axis=-1, keepdims=True)
    h2 = x1 * lax.rsqrt(ms + EPS) * fn_ref[...]
    h2T_ref[...] = h2.T.astype(BF16)


def _merge(y_ret, y_dsa, gates, x, w_ret_o, w_dsa_o, w_out, ffn_norm):
    N, D = x.shape
    tm = TM_MERGE
    tok = lambda width: pl.BlockSpec((tm, width), lambda t: (t, 0))
    full = lambda shape: pl.BlockSpec(shape, lambda t: (0, 0))
    return pl.pallas_call(
        functools.partial(_merge_kernel, d_model=D),
        grid=(N // tm,),
        in_specs=[tok(RET_V), tok(DSA_W), tok(2 * D), tok(D),
                  full((RET_V, D)), full((DSA_W, D)), full((D, D)), full((1, D))],
        out_specs=(tok(D), pl.BlockSpec((D, tm), lambda t: (0, t))),
        out_shape=(jax.ShapeDtypeStruct((N, D), F32), jax.ShapeDtypeStruct((D, N), BF16)),
        compiler_params=_params(("parallel",)),
        name="merge",
    )(y_ret, y_dsa, gates, x, w_ret_o.astype(BF16), w_dsa_o.astype(BF16), w_out.astype(BF16),
      ffn_norm.reshape(1, D))


_ROW_LEN = [PEER_TOPK // (a + 1) for a in range(PEER_TOPK)]


def _extract_rounds(s, n, break_ties):
    rows = lax.broadcasted_iota(I32, s.shape, 0).astype(F32)
    rank = jnp.full(s.shape, float(n), F32)
    vals = []
    for a in range(n):
        m = jnp.max(s, axis=0, keepdims=True)
        hit = s == m
        if break_ties:
            first = jnp.min(jnp.where(hit, rows, float(s.shape[0])), axis=0, keepdims=True)
            hit = rows == first
        s = jnp.where(hit, -jnp.inf, s)
        rank = jnp.where(hit, float(a), rank)
        vals.append(m)
    return vals, rank


def _extract_top_pair(s_a, s_b, n):
    fast = (_extract_rounds(s_a, n, break_ties=False), _extract_rounds(s_b, n, break_ties=False))
    taken = sum(jnp.sum(jnp.where(rank < float(n), 1.0, 0.0), axis=0, keepdims=True) for _, rank in fast)
    return lax.cond(jnp.max(taken) > float(2 * n),
                    lambda: (_extract_rounds(s_a, n, break_ties=True), _extract_rounds(s_b, n, break_ties=True)),
                    lambda: fast)


def _peer_route_kernel(h2T_ref, wqT_ref, sk_ref, r1_ref, p1_ref, n_ref, p0_ref):
    h2T = h2T_ref[...]
    K = PEER_TOPK
    for h in range(PEER_HEADS):
        sT = []
        for c in range(2):
            r0 = (h * 2 + c) * PEER_NKEYS
            qT = jnp.dot(wqT_ref[r0:r0 + PEER_NKEYS, :], h2T, preferred_element_type=F32)
            sT.append(jnp.dot(sk_ref[h, c], qT.astype(BF16), preferred_element_type=F32))
        (v0, rank0), (v1, rank1) = _extract_top_pair(sT[0], sT[1], K)
        cand = [[v0[a] + v1[b] for b in range(_ROW_LEN[a])] for a in range(K)]
        best, _ = _extract_rounds(jnp.concatenate([c for row in cand for c in row], axis=0), K, break_ties=True)
        tau = best[K - 1]
        z = jnp.zeros_like(tau)
        for k in range(K):
            z = z + jnp.exp(best[k] - best[0])
        gt = [sum(jnp.where(c > tau, 1.0, 0.0) for c in row) for row in cand]
        eq = [sum(jnp.where(c == tau, 1.0, 0.0) for c in row) for row in cand]
        need = float(K) - sum(gt)
        n = jnp.zeros(rank0.shape, F32)
        for a in range(K):
            n_a = gt[a] + jnp.minimum(jnp.maximum(need, 0.0), eq[a])
            need = need - eq[a]
            n = jnp.where(rank0 == float(a), n_a, n)
        r1_ref[h] = rank1.astype(BF16)
        p1_ref[h] = jnp.exp(sT[1] - v1[0]).astype(BF16)
        n_ref[h] = n.astype(BF16)
        p0_ref[h] = (0.5 * jnp.exp(sT[0] - v0[0]) / z).astype(BF16)


def _peer_route(h2T, peer_wq, peer_subkeys):
    D, N = h2T.shape
    tm = TM_ROUTE
    wqT = peer_wq.T.astype(BF16)
    sk = peer_subkeys.astype(BF16)
    spec = pl.BlockSpec((PEER_HEADS, PEER_NKEYS, tm), lambda t: (0, 0, t))
    shape = lambda dt: jax.ShapeDtypeStruct((PEER_HEADS, PEER_NKEYS, N), dt)
    return pl.pallas_call(
        _peer_route_kernel,
        grid=(N // tm,),
        in_specs=[pl.BlockSpec((D, tm), lambda t: (0, t)),
                  pl.BlockSpec(wqT.shape, lambda t: (0, 0)),
                  pl.BlockSpec(sk.shape, lambda t: (0, 0, 0, 0))],
        out_specs=(spec, spec, spec, spec),
        out_shape=(shape(BF16), shape(BF16), shape(BF16), shape(BF16)),
        compiler_params=_params(("parallel",)),
        name="peer_route",
    )(h2T, wqT, sk)


def _peer_dense_kernel(h2T_ref, r1_ref, p1_ref, n_ref, p0_ref, u_ref, vT_ref, x1_ref, fn_ref, out_ref,
                       acc_ref, a_ref, g_ref, r1s_ref, p1s_ref, *, n_groups):
    step = pl.program_id(1)
    tm = h2T_ref.shape[1]

    def project(slot):
        piece = u_ref.shape[0] // MXU_PIECES
        for p in range(MXU_PIECES):
            rows = slice(p * piece, (p + 1) * piece)
            a_ref[slot, rows, :] = jnp.dot(u_ref[rows, :], h2T_ref[...], preferred_element_type=F32)

    def gate(slot):
        for c in range(tm // LANES):
            sl = slice(c * LANES, (c + 1) * LANES)
            for il in range(IG_PEER):
                rows = slice(il * PEER_NKEYS, (il + 1) * PEER_NKEYS)
                w = jnp.zeros((PEER_NKEYS, LANES), BF16)
                for h in range(PEER_HEADS):
                    n_i = n_ref[h, il:il + 1, sl]
                    p0 = p0_ref[h, il:il + 1, sl]
                    w = w + jnp.where(r1s_ref[h, :, sl] < n_i, p1s_ref[h, :, sl] * p0, jnp.zeros_like(w))
                a = a_ref[slot, rows, sl]
                gelu2 = a * (1.0 + lax.erf(a * (2.0 ** -0.5)))
                g_ref[slot, rows, sl] = (w.astype(F32) * gelu2).astype(BF16)

    def combine(slot):
        acc_ref[...] += jnp.dot(vT_ref[...], g_ref[slot], preferred_element_type=F32)

    @pl.when(step == 0)
    def _():
        acc_ref[...] = jnp.zeros_like(acc_ref)
        g_ref[1] = jnp.zeros(g_ref.shape[1:], BF16)
        r1s_ref[...] = r1_ref[...]
        p1s_ref[...] = p1_ref[...]
        project(0)

    @pl.when(jnp.logical_and(step > 0, step <= n_groups))
    def _():
        project(step % 2)
        gate((step - 1) % 2)
        combine(step % 2)

    @pl.when(step == n_groups + 1)
    def _():
        combine(step % 2)
        y = x1_ref[...] + acc_ref[...].T
        ms = jnp.mean(y * y, axis=-1, keepdims=True)
        out_ref[...] = y * lax.rsqrt(ms + EPS) * fn_ref[...]


def _peer_dense(h2T, r1, p1, n, p0, peer_u, peer_v, x1, final_norm):
    D, N = h2T.shape
    tm = TM_PEER
    n_exp = peer_u.shape[0]
    u = peer_u.astype(BF16)
    vT = peer_v.T.astype(BF16)
    rows = IG_PEER * PEER_NKEYS
    ng = n_exp // rows
    rt = pl.BlockSpec((PEER_HEADS, PEER_NKEYS, tm), lambda t, s: (0, 0, t))
    grp = pl.BlockSpec((PEER_HEADS, None, IG_PEER, tm), lambda t, s: (0, jnp.clip(s - 1, 0, ng - 1), 0, t))
    n = n.reshape(PEER_HEADS, ng, IG_PEER, N)
    p0 = p0.reshape(PEER_HEADS, ng, IG_PEER, N)
    return pl.pallas_call(
        functools.partial(_peer_dense_kernel, n_groups=ng),
        grid=(N // tm, ng + 2),
        in_specs=[pl.BlockSpec((D, tm), lambda t, s: (0, t)), rt, rt, grp, grp,
                  pl.BlockSpec((rows, D), lambda t, s: (jnp.minimum(s, ng - 1), 0)),
                  pl.BlockSpec((D, rows), lambda t, s: (0, jnp.clip(s - 2, 0, ng - 1))),
                  pl.BlockSpec((tm, D), lambda t, s: (t, 0)),
                  pl.BlockSpec((1, D), lambda t, s: (0, 0))],
        out_specs=pl.BlockSpec((tm, D), lambda t, s: (t, 0)),
        out_shape=jax.ShapeDtypeStruct((N, D), F32),
        scratch_shapes=[pltpu.VMEM((D, tm), F32), pltpu.VMEM((2, rows, tm), F32), pltpu.VMEM((2, rows, tm), BF16),
                        pltpu.VMEM((PEER_HEADS, PEER_NKEYS, tm), BF16), pltpu.VMEM((PEER_HEADS, PEER_NKEYS, tm), BF16)],
        compiler_params=_params(("parallel", "arbitrary")),
        name="peer_dense",
    )(h2T, r1, p1, n, p0, u, vT, x1, final_norm.reshape(1, D))


def kernel(x, attn_norm, w_in, ret_gn, w_ret_o, w_dsa_o, w_out, ffn_norm, peer_wq, peer_subkeys, peer_u, peer_v,
           final_norm):
    B, L, D = x.shape
    assert attn_norm.shape[0] == 1, "single-layer block"
    assert L % TM_IN == 0 and (B * L) % TM_PEER == 0 and D % LANES == 0 and QB == LANES
    rq, rk, rv, rg, dqT, dk, dvT, iqT, ik, iwT, gates = _inproj(x, attn_norm[0], w_in[0])
    y_ret = _retention(rq, rk, rv, rg, ret_gn[0])
    y_dsa = _dsa(iqT, iwT, ik, dqT, dk, dvT)
    x1, h2T = _merge(y_ret.reshape(B * L, RET_V), y_dsa.reshape(B * L, DSA_W), gates.reshape(B * L, 2 * D),
                     x.reshape(B * L, D), w_ret_o[0], w_dsa_o[0], w_out[0], ffn_norm[0])
    r1, p1, n, p0 = _peer_route(h2T, peer_wq[0], peer_subkeys[0])
    out = _peer_dense(h2T, r1, p1, n, p0, peer_u[0], peer_v[0], x1, final_norm)
    return out.reshape(B, L, D)
```

```python
import functools

import jax
import jax.numpy as jnp
import numpy as np
from jax import lax
from jax.experimental import pallas as pl
from jax.experimental.pallas import tpu as pltpu

F32 = jnp.float32
BF16 = jnp.bfloat16
I32 = jnp.int32

CHUNK = 64
EPS = 1e-6
RET_HEADS, RET_DK, RET_DV, RET_THETA = 4, 64, 128, 10000.0
DSA_HEADS, DSA_DH, IDX_HEADS, IDX_DH = 8, 64, 4, 64
DSA_TOPK_MAX = 256
ROPE_THETA, ROPE_DIM = 500000.0, 16
PEER_HEADS, PEER_NKEYS, PEER_TOPK = 8, 128, 16

LANES = 128
SUBLANES = 8
VMEM_LIMIT = 56 * 1024 * 1024

RET_QK = RET_HEADS * RET_DK
RET_V = RET_HEADS * RET_DV
DSA_W = DSA_HEADS * DSA_DH
IDX_Q = IDX_HEADS * IDX_DH

TM_IN = 512
TB_RET = 256
QB = 128
KB = 256
SM_ROWS = 64
TM_MERGE = 512
TM_ROUTE = 256
TM_PEER = 512
IG_PEER = 16
MXU_PIECES = 4

NEG_BIG = -1e30
M_INIT = -1e20
INT_MIN = -(2 ** 31)

_OFF = {}
_o = 0
for _name, _w in (("rq", RET_QK), ("rk", RET_QK), ("rv", RET_V), ("rg", RET_V), ("dq", DSA_W), ("dk", DSA_W),
                  ("dv", DSA_W), ("iq", IDX_Q), ("ik", LANES), ("iw", LANES), ("gates", None)):
    _OFF[_name] = _o
    if _w is not None:
        _o += _w


def _params(sem):
    return pltpu.CompilerParams(dimension_semantics=sem, vmem_limit_bytes=VMEM_LIMIT)


def _rope_chunk(p, cos, sin, half):
    lane = lax.broadcasted_iota(I32, p.shape, 1) % 64
    partner = jnp.where(lane < half, pltpu.roll(p, LANES - half, 1), pltpu.roll(p, half, 1))
    return p * cos + partner * sin


def _inproj_kernel(x_ref, g_ref, w_ref, cr_ref, sr_ref, cp_ref, sp_ref,
                   rq_ref, rk_ref, rv_ref, rg_ref, dqT_ref, dk_ref, dvT_ref, iqT_ref, ik_ref, iwT_ref, gate_ref,
                   *, d_model):
    x = x_ref[...]
    ms = jnp.mean(x * x, axis=-1, keepdims=True)
    h = (x * lax.rsqrt(ms + EPS) * g_ref[...]).astype(BF16)

    def proj(off, width):
        return jnp.dot(h, w_ref[:, off:off + width], preferred_element_type=F32)

    cr, sr, cp, sp = cr_ref[...], sr_ref[...], cp_ref[...], sp_ref[...]
    chunk = lambda p, c: p[:, c * LANES:(c + 1) * LANES]
    qk_scale = RET_DK ** -0.5
    p_rqk = proj(_OFF["rq"], 2 * RET_QK)
    for c in range(RET_QK // LANES):
        sl = slice(c * LANES, (c + 1) * LANES)
        rq_ref[:, sl] = _rope_chunk(chunk(p_rqk, c), cr, sr, RET_DK // 2) * qk_scale
        rk_ref[:, sl] = _rope_chunk(chunk(p_rqk, RET_QK // LANES + c), cr, sr, RET_DK // 2)
    rv_ref[...] = proj(_OFF["rv"], RET_V).astype(BF16)
    rg_ref[...] = proj(_OFF["rg"], RET_V)
    tm = x.shape[0]
    dh_scale = DSA_DH ** -0.5
    p_dq, p_dk, p_dv = proj(_OFF["dq"], DSA_W), proj(_OFF["dk"], DSA_W), proj(_OFF["dv"], DSA_W)
    for c in range(DSA_W // LANES):
        sl = slice(c * LANES, (c + 1) * LANES)
        dq = _rope_chunk(chunk(p_dq, c), cp, sp, ROPE_DIM // 2) * dh_scale
        dqT_ref[sl, :] = dq.T.astype(BF16)
        dk_ref[:, sl] = _rope_chunk(chunk(p_dk, c), cp, sp, ROPE_DIM // 2).astype(BF16)
        dv = chunk(p_dv, c)
        for j in range(tm // KB):
            dvT_ref[j, sl, :] = dv[j * KB:(j + 1) * KB, :].T.astype(BF16)
    p_iq = proj(_OFF["iq"], IDX_Q)
    for c in range(IDX_Q // LANES):
        iq = _rope_chunk(chunk(p_iq, c), cp, sp, ROPE_DIM // 2)
        for j in range(tm // QB):
            t = iq[j * QB:(j + 1) * QB, :].T.astype(BF16)
            iqT_ref[j, :, (2 * c) * QB:(2 * c + 1) * QB] = t[:IDX_DH, :]
            iqT_ref[j, :, (2 * c + 1) * QB:(2 * c + 2) * QB] = t[IDX_DH:, :]
    p_ikw = proj(_OFF["ik"], 2 * LANES)
    ik = _rope_chunk(chunk(p_ikw, 0), cp, sp, ROPE_DIM // 2)
    ik_ref[...] = ik[:, :IDX_DH].astype(BF16)
    iw = chunk(p_ikw, 1) * ((IDX_HEADS ** -0.5) * (IDX_DH ** -0.5))
    iwT_ref[...] = iw.T[:SUBLANES, :]
    gate_ref[...] = proj(_OFF["gates"], 2 * d_model)


def _rope_tables(L, half, theta, pass_dims):
    inv = theta ** (-jnp.arange(half, dtype=F32) / half)
    ang = jnp.arange(L).astype(F32)[:, None] * inv[None, :]
    cos, sin = jnp.cos(ang), jnp.sin(ang)
    ones = jnp.ones((L, pass_dims), F32)
    zeros = jnp.zeros((L, pass_dims), F32)
    cos64 = jnp.concatenate([cos, cos, ones], axis=-1)
    sin64 = jnp.concatenate([-sin, sin, zeros], axis=-1)
    return jnp.tile(cos64, (1, 2)), jnp.tile(sin64, (1, 2))


def _inproj(x, attn_norm, w_in):
    B, L, D = x.shape
    tm = TM_IN
    nl = L // tm
    sizes = (RET_QK, RET_QK, RET_V, RET_V, DSA_W, DSA_W, DSA_W, IDX_Q, IDX_DH, IDX_HEADS, 2 * D)
    offs = np.concatenate([[0], np.cumsum(sizes)])
    cols = [w_in[:, offs[i]:offs[i + 1]] for i in range(len(sizes))]
    cols[8] = jnp.pad(cols[8], ((0, 0), (0, LANES - IDX_DH)))
    cols[9] = jnp.pad(cols[9], ((0, 0), (0, LANES - IDX_HEADS)))
    w = jnp.concatenate(cols, axis=1).astype(BF16)
    wtot = w.shape[1]
    cr, sr = _rope_tables(L, RET_DK // 2, RET_THETA, 0)
    cp, sp = _rope_tables(L, ROPE_DIM // 2, ROPE_THETA, DSA_DH - ROPE_DIM)

    tok = lambda width: pl.BlockSpec((None, tm, width), lambda b, l: (b, l, 0))
    tab = pl.BlockSpec((tm, LANES), lambda b, l: (l, 0))
    out_shape = (
        jax.ShapeDtypeStruct((B, L, RET_QK), F32),
        jax.ShapeDtypeStruct((B, L, RET_QK), F32),
        jax.ShapeDtypeStruct((B, L, RET_V), BF16),
        jax.ShapeDtypeStruct((B, L, RET_V), F32),
        jax.ShapeDtypeStruct((B, DSA_W, L), BF16),
        jax.ShapeDtypeStruct((B, L, DSA_W), BF16),
        jax.ShapeDtypeStruct((B, L // KB, DSA_W, KB), BF16),
        jax.ShapeDtypeStruct((B, L // QB, IDX_DH, IDX_HEADS * QB), BF16),
        jax.ShapeDtypeStruct((B, L, IDX_DH), BF16),
        jax.ShapeDtypeStruct((B, SUBLANES, L), F32),
        jax.ShapeDtypeStruct((B, L, 2 * D), F32),
    )
    out_specs = (
        tok(RET_QK), tok(RET_QK), tok(RET_V), tok(RET_V),
        pl.BlockSpec((None, DSA_W, tm), lambda b, l: (b, 0, l)),
        tok(DSA_W),
        pl.BlockSpec((None, tm // KB, DSA_W, KB), lambda b, l: (b, l, 0, 0)),
        pl.BlockSpec((None, tm // QB, IDX_DH, IDX_HEADS * QB), lambda b, l: (b, l, 0, 0)),
        tok(IDX_DH),
        pl.BlockSpec((None, SUBLANES, tm), lambda b, l: (b, 0, l)),
        tok(2 * D),
    )
    return pl.pallas_call(
        functools.partial(_inproj_kernel, d_model=D),
        grid=(B, nl),
        in_specs=[tok(D), pl.BlockSpec((1, D), lambda b, l: (0, 0)),
                  pl.BlockSpec((D, wtot), lambda b, l: (0, 0)), tab, tab, tab, tab],
        out_specs=out_specs,
        out_shape=out_shape,
        compiler_params=_params(("parallel", "parallel")),
        name="inproj",
    )(x, attn_norm.reshape(1, D), w, cr, sr, cp, sp)


def _retention_kernel(q_ref, k_ref, v_ref, g_ref, gn_ref, dm_ref, qd_ref, kd_ref, cd_ref, y_ref, state_ref):
    @pl.when(pl.program_id(1) == 0)
    def _():
        state_ref[...] = jnp.zeros_like(state_ref)

    for h in range(RET_HEADS):
        q = q_ref[:, h * RET_DK:(h + 1) * RET_DK]
        k = k_ref[:, h * RET_DK:(h + 1) * RET_DK]
        v = v_ref[:, h * RET_DV:(h + 1) * RET_DV]
        s = lax.dot_general(q.astype(BF16), k.astype(BF16), (((1,), (1,)), ((), ())), preferred_element_type=F32)
        intra = jnp.dot((s * dm_ref[h]).astype(BF16), v, preferred_element_type=F32)
        state = state_ref[h]
        cross = jnp.dot((q * qd_ref[h]).astype(BF16), state.astype(BF16), preferred_element_type=F32)
        kv = lax.dot_general((k * kd_ref[h]).astype(BF16), v, (((0,), (0,)), ((), ())), preferred_element_type=F32)
        state_ref[h] = cd_ref[h] * state + kv
        y = intra + cross
        mu = jnp.mean(y, axis=-1, keepdims=True)
        yc = y - mu
        var = jnp.mean(yc * yc, axis=-1, keepdims=True)
        sl = slice(h * RET_DV, (h + 1) * RET_DV)
        yn = yc * lax.rsqrt(var + EPS) * gn_ref[:, sl]
        g = g_ref[:, sl]
        y_ref[:, sl] = (g * jax.nn.sigmoid(g) * yn).astype(y_ref.dtype)


def _retention(rq, rk, rv, rg, ret_gn):
    B, L, _ = rq.shape
    tb = TB_RET
    log_g = jnp.log(1.0 - 2.0 ** (-5.0 - jnp.arange(RET_HEADS, dtype=F32)))
    i = jnp.arange(tb, dtype=F32)
    ci = jnp.arange(tb) // CHUNK
    visible = (ci[None, :] <= ci[:, None]).astype(F32)
    dm = jnp.exp(log_g[:, None, None] * jnp.abs(i[:, None] - i[None, :])) * visible[None]
    ones = jnp.ones((1, 1, RET_DK), F32)
    qd = jnp.exp(log_g[:, None] * (i + 1.0)[None, :])[:, :, None] * ones
    kd = jnp.exp(log_g[:, None] * (tb - 1.0 - i)[None, :])[:, :, None] * ones
    cd = jnp.exp(log_g * tb)[:, None, None] * jnp.ones((1, RET_DK, RET_DV), F32)

    tok = lambda width: pl.BlockSpec((None, tb, width), lambda b, l: (b, l, 0))
    full = lambda shape: pl.BlockSpec(shape, lambda b, l: (0,) * len(shape))
    return pl.pallas_call(
        _retention_kernel,
        grid=(B, L // tb),
        in_specs=[tok(RET_QK), tok(RET_QK), tok(RET_V), tok(RET_V), full((1, RET_V)),
                  full((RET_HEADS, tb, tb)), full((RET_HEADS, tb, RET_DK)), full((RET_HEADS, tb, RET_DK)),
                  full((RET_HEADS, RET_DK, RET_DV))],
        out_specs=tok(RET_V),
        out_shape=jax.ShapeDtypeStruct((B, L, RET_V), BF16),
        scratch_shapes=[pltpu.VMEM((RET_HEADS, RET_DK, RET_DV), F32)],
        compiler_params=_params(("parallel", "arbitrary")),
        name="retention",
    )(rq, rk, rv, rg, ret_gn.reshape(1, RET_V), dm, qd, kd, cd)


def _dsa_kernel(iqT_ref, iwT_ref, ik_ref, dqT_ref, dk_ref, dvT_ref, tri_ref, y_ref,
                keys_ref, rhs_ref, m_ref, l_ref, acc_ref, s_ref, p_ref, alpha_ref, smax_ref, *, topk):
    qb = pl.program_id(1)
    nkb = (qb * QB) // KB + 1
    n_trips = (nkb + 1) // 2
    n_pairs = DSA_HEADS // 2

    key_row = lax.broadcasted_iota(I32, (KB, QB), 0)
    q_chunk = (qb * QB + lax.broadcasted_iota(I32, (KB, QB), 1)) // CHUNK

    def score_block(kb):
        k0 = pl.multiple_of(kb * KB, KB)
        r = jnp.dot(ik_ref[pl.ds(k0, KB), :], iqT_ref[...], preferred_element_type=F32)
        acc = jnp.zeros((KB, QB), F32)
        for h in range(IDX_HEADS):
            acc = acc + iwT_ref[h:h + 1, :] * jnp.maximum(r[:, h * QB:(h + 1) * QB], 0.0)
        acc = acc + 0.0
        bits = pltpu.bitcast(acc, I32)
        key = jnp.where(bits < 0, bits ^ 0x7FFFFFFF, bits)
        admissible = (k0 + key_row) // CHUNK <= q_chunk
        keys_ref[kb] = jnp.where(admissible, key, INT_MIN)

    def score_two_blocks(t, carry):
        score_block(2 * t)
        score_block(2 * t + 1)
        return carry

    lax.fori_loop(0, n_trips, score_two_blocks, 0)

    def count_ge(cand):
        def body(t, cnt):
            for kb in (2 * t, 2 * t + 1):
                hit = jnp.where(keys_ref[kb] >= cand, 1.0, 0.0)
                cnt = cnt + jnp.sum(hit.reshape(KB // SM_ROWS, SM_ROWS, QB), axis=0)
            return cnt

        cnt = lax.fori_loop(0, n_trips, body, jnp.zeros((SM_ROWS, QB), F32))
        return jnp.sum(cnt, axis=0, keepdims=True).astype(I32)

    zero = jnp.zeros((1, QB), I32)
    t0 = jnp.where(count_ge(zero) >= topk, zero, zero + INT_MIN)

    def bit_step(n, t):
        cand = t + jnp.left_shift(jnp.int32(1), 30 - n)
        return jnp.where(count_ge(cand) >= topk, cand, t)

    thr = lax.fori_loop(0, 31, bit_step, t0)
    n_gt = count_ge(thr + 1)
    need = jnp.where(thr == INT_MIN, 0, topk - n_gt).astype(F32)

    d_row = lax.broadcasted_iota(I32, (LANES, QB), 0)
    eye = jnp.where(d_row == lax.broadcasted_iota(I32, (LANES, QB), 1), 1.0, 0.0).astype(BF16)
    for j in range(n_pairs):
        qT = dqT_ref[j * LANES:(j + 1) * LANES, :]
        rhs_ref[j, :LANES, :QB] = jnp.where(d_row < DSA_DH, qT, jnp.zeros_like(qT))
        rhs_ref[j, :LANES, QB:] = jnp.where(d_row >= DSA_DH, qT, jnp.zeros_like(qT))
        rhs_ref[j, LANES:, :QB] = eye
        rhs_ref[j, LANES:, QB:] = eye
    m_ref[...] = jnp.full(m_ref.shape, M_INIT, F32)
    l_ref[...] = jnp.zeros(l_ref.shape, F32)
    acc_ref[...] = jnp.zeros(acc_ref.shape, F32)

    def logits(blk, slot, n_eq_before):
        kb = jnp.minimum(blk, nkb - 1)
        k0 = pl.multiple_of(kb * KB, KB)
        key = jnp.where(blk < nkb, keys_ref[kb], INT_MIN)
        eq = key == thr
        eq_f = jnp.where(eq, 1.0, 0.0)
        prefix = jnp.dot(tri_ref[...], eq_f.astype(BF16), preferred_element_type=F32)
        keep_tie = (n_eq_before + prefix) < need
        bias = jnp.where(key > thr, 0.0, jnp.where(eq, jnp.where(keep_tie, 0.0, NEG_BIG), NEG_BIG)).astype(BF16)
        for j in range(n_pairs):
            lhs = jnp.concatenate([dk_ref[pl.ds(k0, KB), j * LANES:(j + 1) * LANES], bias], axis=1)
            s = jnp.dot(lhs, rhs_ref[j], preferred_element_type=F32)
            s_ref[slot, j] = s
            smax_ref[slot, j] = jnp.max(s, axis=0, keepdims=True)
        return n_eq_before + jnp.sum(eq_f, axis=0, keepdims=True)

    def softmax_update(slot):
        rows = [slice(r * SM_ROWS, (r + 1) * SM_ROWS) for r in range(KB // SM_ROWS)]
        for j in range(n_pairs):
            m_prev = m_ref[j]
            m_new = jnp.maximum(m_prev, smax_ref[slot, j])
            alpha = jnp.exp(m_prev - m_new)
            psum = jnp.zeros((SM_ROWS, 2 * QB), F32)
            for r in rows:
                p = jnp.exp(s_ref[slot, j, r, :] - m_new)
                psum = psum + p
                p_ref[slot, j, r, :] = p.astype(BF16)
            l_ref[j] = alpha * l_ref[j] + jnp.sum(psum, axis=0, keepdims=True)
            m_ref[j] = m_new
            alpha_ref[slot, j] = alpha

    def accumulate(kb, slot):
        for j in range(n_pairs):
            pv = jnp.dot(dvT_ref[kb, j * LANES:(j + 1) * LANES, :], p_ref[slot, j], preferred_element_type=F32)
            acc_ref[j] = alpha_ref[slot, j] * acc_ref[j] + pv

    p_ref[1] = jnp.zeros(p_ref.shape[1:], BF16)
    alpha_ref[1] = jnp.ones(alpha_ref.shape[1:], F32)
    n_eq0 = logits(0, 0, jnp.zeros((1, QB), F32))

    def attend_two_blocks(t, n_eq):
        b = 2 * t
        n_eq = logits(b + 1, 1, n_eq)
        softmax_update(0)
        accumulate(jnp.maximum(b - 1, 0), 1)
        n_eq = logits(b + 2, 0, n_eq)
        softmax_update(1)
        accumulate(b, 0)
        return n_eq

    lax.fori_loop(0, n_trips, attend_two_blocks, n_eq0)
    accumulate(jnp.minimum(2 * n_trips - 1, nkb - 1), 1)

    outs = []
    for j in range(n_pairs):
        acc, l = acc_ref[j], l_ref[j]
        outs.append(acc[:DSA_DH, :QB] / l[:, :QB])
        outs.append(acc[DSA_DH:, QB:] / l[:, QB:])
    y_ref[...] = jnp.concatenate(outs, axis=0).T.astype(y_ref.dtype)


def _dsa(iqT, iwT, ik, dqT, dk, dvT):
    B, L, _ = dk.shape
    topk = min(DSA_TOPK_MAX, L // 4)
    r = jnp.arange(KB)
    tri = (r[None, :] < r[:, None]).astype(BF16)
    per_batch = lambda shape: pl.BlockSpec((None,) + shape, lambda b, q: (b,) + (0,) * len(shape))
    return pl.pallas_call(
        functools.partial(_dsa_kernel, topk=topk),
        grid=(B, L // QB),
        in_specs=[pl.BlockSpec((None, None, IDX_DH, IDX_HEADS * QB), lambda b, q: (b, q, 0, 0)),
                  pl.BlockSpec((None, SUBLANES, QB), lambda b, q: (b, 0, q)),
                  per_batch((L, IDX_DH)),
                  pl.BlockSpec((None, DSA_W, QB), lambda b, q: (b, 0, q)),
                  per_batch((L, DSA_W)),
                  per_batch((L // KB, DSA_W, KB)),
                  pl.BlockSpec((KB, KB), lambda b, q: (0, 0))],
        out_specs=pl.BlockSpec((None, QB, DSA_W), lambda b, q: (b, q, 0)),
        out_shape=jax.ShapeDtypeStruct((B, L, DSA_W), BF16),
        scratch_shapes=[pltpu.VMEM((L // KB, KB, QB), I32),
                        pltpu.VMEM((DSA_HEADS // 2, 2 * LANES, 2 * QB), BF16),
                        pltpu.VMEM((DSA_HEADS // 2, 1, 2 * QB), F32),
                        pltpu.VMEM((DSA_HEADS // 2, 1, 2 * QB), F32),
                        pltpu.VMEM((DSA_HEADS // 2, LANES, 2 * QB), F32),
                        pltpu.VMEM((2, DSA_HEADS // 2, KB, 2 * QB), F32),
                        pltpu.VMEM((2, DSA_HEADS // 2, KB, 2 * QB), BF16),
                        pltpu.VMEM((2, DSA_HEADS // 2, 1, 2 * QB), F32),
                        pltpu.VMEM((2, DSA_HEADS // 2, 1, 2 * QB), F32)],
        compiler_params=_params(("parallel", "arbitrary")),
        name="dsa",
    )(iqT, iwT, ik, dqT, dk, dvT, tri)


def _merge_kernel(yr_ref, yd_ref, gate_ref, x_ref, wr_ref, wd_ref, wo_ref, fn_ref, x1_ref, h2T_ref, *, d_model):
    g = gate_ref[...]
    br = jnp.dot(yr_ref[...], wr_ref[...], preferred_element_type=F32)
    bd = jnp.dot(yd_ref[...], wd_ref[...], preferred_element_type=F32)
    merged = jax.nn.sigmoid(g[:, :d_model]) * br + jax.nn.sigmoid(g[:, d_model:]) * bd
    x1 = x_ref[...] + jnp.dot(merged.astype(BF16), wo_ref[...], preferred_element_type=F32)
    x1_ref[...] = x1
    ms = jnp.mean(x1 * x1, axis=-1, keepdims=True)
    h2 = x1 * lax.rsqrt(ms + EPS) * fn_ref[...]
    h2T_ref[...] = h2.T.astype(BF16)


def _merge(y_ret, y_dsa, gates, x, w_ret_o, w_dsa_o, w_out, ffn_norm):
    N, D = x.shape
    tm = TM_MERGE
    tok = lambda width: pl.BlockSpec((tm, width), lambda t: (t, 0))
    full = lambda shape: pl.BlockSpec(shape, lambda t: (0, 0))
    return pl.pallas_call(
        functools.partial(_merge_kernel, d_model=D),
        grid=(N // tm,),
        in_specs=[tok(RET_V), tok(DSA_W), tok(2 * D), tok(D),
                  full((RET_V, D)), full((DSA_W, D)), full((D, D)), full((1, D))],
        out_specs=(tok(D), pl.BlockSpec((D, tm), lambda t: (0, t))),
        out_shape=(jax.ShapeDtypeStruct((N, D), F32), jax.ShapeDtypeStruct((D, N), BF16)),
        compiler_params=_params(("parallel",)),
        name="merge",
    )(y_ret, y_dsa, gates, x, w_ret_o.astype(BF16), w_dsa_o.astype(BF16), w_out.astype(BF16),
      ffn_norm.reshape(1, D))


_ROW_LEN = [PEER_TOPK // (a + 1) for a in range(PEER_TOPK)]


def _extract_rounds(s, n, break_ties):
    rows = lax.broadcasted_iota(I32, s.shape, 0).astype(F32)
    rank = jnp.full(s.shape, float(n), F32)
    vals = []
    for a in range(n):
        m = jnp.max(s, axis=0, keepdims=True)
        hit = s == m
        if break_ties:
            first = jnp.min(jnp.where(hit, rows, float(s.shape[0])), axis=0, keepdims=True)
            hit = rows == first
        s = jnp.where(hit, -jnp.inf, s)
        rank = jnp.where(hit, float(a), rank)
        vals.append(m)
    return vals, rank


def _extract_top_pair(s_a, s_b, n):
    fast = (_extract_rounds(s_a, n, break_ties=False), _extract_rounds(s_b, n, break_ties=False))
    taken = sum(jnp.sum(jnp.where(rank < float(n), 1.0, 0.0), axis=0, keepdims=True) for _, rank in fast)
    return lax.cond(jnp.max(taken) > float(2 * n),
                    lambda: (_extract_rounds(s_a, n, break_ties=True), _extract_rounds(s_b, n, break_ties=True)),
                    lambda: fast)


def _peer_route_kernel(h2T_ref, wqT_ref, sk_ref, r1_ref, p1_ref, n_ref, p0_ref):
    h2T = h2T_ref[...]
    K = PEER_TOPK
    for h in range(PEER_HEADS):
        sT = []
        for c in range(2):
            r0 = (h * 2 + c) * PEER_NKEYS
            qT = jnp.dot(wqT_ref[r0:r0 + PEER_NKEYS, :], h2T, preferred_element_type=F32)
            sT.append(jnp.dot(sk_ref[h, c], qT.astype(BF16), preferred_element_type=F32))
        (v0, rank0), (v1, rank1) = _extract_top_pair(sT[0], sT[1], K)
        cand = [[v0[a] + v1[b] for b in range(_ROW_LEN[a])] for a in range(K)]
        best, _ = _extract_rounds(jnp.concatenate([c for row in cand for c in row], axis=0), K, break_ties=True)
        tau = best[K - 1]
        z = jnp.zeros_like(tau)
        for k in range(K):
            z = z + jnp.exp(best[k] - best[0])
        gt = [sum(jnp.where(c > tau, 1.0, 0.0) for c in row) for row in cand]
        eq = [sum(jnp.where(c == tau, 1.0, 0.0) for c in row) for row in cand]
        need = float(K) - sum(gt)
        n = jnp.zeros(rank0.shape, F32)
        for a in range(K):
            n_a = gt[a] + jnp.minimum(jnp.maximum(need, 0.0), eq[a])
            need = need - eq[a]
            n = jnp.where(rank0 == float(a), n_a, n)
        r1_ref[h] = rank1.astype(BF16)
        p1_ref[h] = jnp.exp(sT[1] - v1[0]).astype(BF16)
        n_ref[h] = n.astype(BF16)
        p0_ref[h] = (0.5 * jnp.exp(sT[0] - v0[0]) / z).astype(BF16)


def _peer_route(h2T, peer_wq, peer_subkeys):
    D, N = h2T.shape
    tm = TM_ROUTE
    wqT = peer_wq.astype(BF16).T
    sk = peer_subkeys.astype(BF16)
    spec = pl.BlockSpec((PEER_HEADS, PEER_NKEYS, tm), lambda t: (0, 0, t))
    shape = lambda dt: jax.ShapeDtypeStruct((PEER_HEADS, PEER_NKEYS, N), dt)
    return pl.pallas_call(
        _peer_route_kernel,
        grid=(N // tm,),
        in_specs=[pl.BlockSpec((D, tm), lambda t: (0, t)),
                  pl.BlockSpec(wqT.shape, lambda t: (0, 0)),
                  pl.BlockSpec(sk.shape, lambda t: (0, 0, 0, 0))],
        out_specs=(spec, spec, spec, spec),
        out_shape=(shape(BF16), shape(BF16), shape(BF16), shape(BF16)),
        compiler_params=_params(("parallel",)),
        name="peer_route",
    )(h2T, wqT, sk)


def _peer_dense_kernel(h2T_ref, r1_ref, p1_ref, n_ref, p0_ref, u_ref, vT_ref, x1_ref, fn_ref, out_ref,
                       acc_ref, a_ref, g_ref, r1s_ref, p1s_ref, *, n_groups):
    step = pl.program_id(1)
    tm = h2T_ref.shape[1]

    def project(slot):
        piece = u_ref.shape[0] // MXU_PIECES
        for p in range(MXU_PIECES):
            rows = slice(p * piece, (p + 1) * piece)
            a_ref[slot, rows, :] = jnp.dot(u_ref[rows, :], h2T_ref[...], preferred_element_type=F32)

    def gate(slot):
        for c in range(tm // LANES):
            sl = slice(c * LANES, (c + 1) * LANES)
            for il in range(IG_PEER):
                rows = slice(il * PEER_NKEYS, (il + 1) * PEER_NKEYS)
                w = jnp.zeros((PEER_NKEYS, LANES), BF16)
                for h in range(PEER_HEADS):
                    n_i = n_ref[h, il:il + 1, sl]
                    p0 = p0_ref[h, il:il + 1, sl]
                    w = w + jnp.where(r1s_ref[h, :, sl] < n_i, p1s_ref[h, :, sl] * p0, jnp.zeros_like(w))
                a = a_ref[slot, rows, sl]
                gelu2 = a * (1.0 + lax.erf(a * (2.0 ** -0.5)))
                g_ref[slot, rows, sl] = (w.astype(F32) * gelu2).astype(BF16)

    def combine(slot):
        acc_ref[...] += jnp.dot(vT_ref[...], g_ref[slot], preferred_element_type=F32)

    @pl.when(step == 0)
    def _():
        acc_ref[...] = jnp.zeros_like(acc_ref)
        g_ref[1] = jnp.zeros(g_ref.shape[1:], BF16)
        r1s_ref[...] = r1_ref[...]
        p1s_ref[...] = p1_ref[...]
        project(0)

    @pl.when(jnp.logical_and(step > 0, step <= n_groups))
    def _():
        project(step % 2)
        gate((step - 1) % 2)
        combine(step % 2)

    @pl.when(step == n_groups + 1)
    def _():
        combine(step % 2)
        y = x1_ref[...] + acc_ref[...].T
        ms = jnp.mean(y * y, axis=-1, keepdims=True)
        out_ref[...] = y * lax.rsqrt(ms + EPS) * fn_ref[...]


def _peer_dense(h2T, r1, p1, n, p0, peer_u, peer_v, x1, final_norm):
    D, N = h2T.shape
    tm = TM_PEER
    n_exp = peer_u.shape[0]
    u = peer_u.astype(BF16)
    vT = peer_v.astype(BF16).T
    rows = IG_PEER * PEER_NKEYS
    ng = n_exp // rows
    rt = pl.BlockSpec((PEER_HEADS, PEER_NKEYS, tm), lambda t, s: (0, 0, t))
    grp = pl.BlockSpec((PEER_HEADS, None, IG_PEER, tm), lambda t, s: (0, jnp.clip(s - 1, 0, ng - 1), 0, t))
    n = n.reshape(PEER_HEADS, ng, IG_PEER, N)
    p0 = p0.reshape(PEER_HEADS, ng, IG_PEER, N)
    return pl.pallas_call(
        functools.partial(_peer_dense_kernel, n_groups=ng),
        grid=(N // tm, ng + 2),
        in_specs=[pl.BlockSpec((D, tm), lambda t, s: (0, t)), rt, rt, grp, grp,
                  pl.BlockSpec((rows, D), lambda t, s: (jnp.minimum(s, ng - 1), 0)),
                  pl.BlockSpec((D, rows), lambda t, s: (0, jnp.clip(s - 2, 0, ng - 1))),
                  pl.BlockSpec((tm, D), lambda t, s: (t, 0)),
                  pl.BlockSpec((1, D), lambda t, s: (0, 0))],
        out_specs=pl.BlockSpec((tm, D), lambda t, s: (t, 0)),
        out_shape=jax.ShapeDtypeStruct((N, D), F32),
        scratch_shapes=[pltpu.VMEM((D, tm), F32), pltpu.VMEM((2, rows, tm), F32), pltpu.VMEM((2, rows, tm), BF16),
                        pltpu.VMEM((PEER_HEADS, PEER_NKEYS, tm), BF16), pltpu.VMEM((PEER_HEADS, PEER_NKEYS, tm), BF16)],
        compiler_params=_params(("parallel", "arbitrary")),
        name="peer_dense",
    )(h2T, r1, p1, n, p0, u, vT, x1, final_norm.reshape(1, D))


def kernel(x, attn_norm, w_in, ret_gn, w_ret_o, w_dsa_o, w_out, ffn_norm, peer_wq, peer_subkeys, peer_u, peer_v,
           final_norm):
    B, L, D = x.shape
    assert attn_norm.shape[0] == 1, "single-layer block"
    assert L % TM_IN == 0 and (B * L) % TM_PEER == 0 and D % LANES == 0 and QB == LANES
    rq, rk, rv, rg, dqT, dk, dvT, iqT, ik, iwT, gates = _inproj(x, attn_norm[0], w_in[0])
    y_ret = _retention(rq, rk, rv, rg, ret_gn[0])
    y_dsa = _dsa(iqT, iwT, ik, dqT, dk, dvT)
    x1, h2T = _merge(y_ret.reshape(B * L, RET_V), y_dsa.reshape(B * L, DSA_W), gates.reshape(B * L, 2 * D),
                     x.reshape(B * L, D), w_ret_o[0], w_dsa_o[0], w_out[0], ffn_norm[0])
    r1, p1, n, p0 = _peer_route(h2T, peer_wq[0], peer_subkeys[0])
    out = _peer_dense(h2T, r1, p1, n, p0, peer_u[0], peer_v[0], x1, final_norm)
    return out.reshape(B, L, D)
```

```python
import functools

import jax
import jax.numpy as jnp
import numpy as np
from jax import lax
from jax.experimental import pallas as pl
from jax.experimental.pallas import tpu as pltpu

F32 = jnp.float32
BF16 = jnp.bfloat16
I32 = jnp.int32

CHUNK = 64
EPS = 1e-6
RET_HEADS, RET_DK, RET_DV, RET_THETA = 4, 64, 128, 10000.0
DSA_HEADS, DSA_DH, IDX_HEADS, IDX_DH = 8, 64, 4, 64
DSA_TOPK_MAX = 256
ROPE_THETA, ROPE_DIM = 500000.0, 16
PEER_HEADS, PEER_NKEYS, PEER_TOPK = 8, 128, 16

LANES = 128
SUBLANES = 8
VMEM_LIMIT = 56 * 1024 * 1024

RET_QK = RET_HEADS * RET_DK
RET_V = RET_HEADS * RET_DV
DSA_W = DSA_HEADS * DSA_DH
IDX_Q = IDX_HEADS * IDX_DH

TM_IN = 512
TB_RET = 256
QB = 128
KB = 256
SM_ROWS = 64
TM_MERGE = 512
TM_ROUTE = 256
TM_PEER = 512
IG_PEER = 16
MXU_PIECES = 2

NEG_BIG = -1e30
M_INIT = -1e20
INT_MIN = -(2 ** 31)

_OFF = {}
_o = 0
for _name, _w in (("rq", RET_QK), ("rk", RET_QK), ("rv", RET_V), ("rg", RET_V), ("dq", DSA_W), ("dk", DSA_W),
                  ("dv", DSA_W), ("iq", IDX_Q), ("ik", LANES), ("iw", LANES), ("gates", None)):
    _OFF[_name] = _o
    if _w is not None:
        _o += _w


def _params(sem):
    return pltpu.CompilerParams(dimension_semantics=sem, vmem_limit_bytes=VMEM_LIMIT)


def _rope_chunk(p, cos, sin, half):
    lane = lax.broadcasted_iota(I32, p.shape, 1) % 64
    partner = jnp.where(lane < half, pltpu.roll(p, LANES - half, 1), pltpu.roll(p, half, 1))
    return p * cos + partner * sin


def _inproj_kernel(x_ref, g_ref, w_ref, cr_ref, sr_ref, cp_ref, sp_ref,
                   rq_ref, rk_ref, rv_ref, rg_ref, dqT_ref, dk_ref, dvT_ref, iqT_ref, ik_ref, iwT_ref, gate_ref,
                   *, d_model):
    x = x_ref[...]
    ms = jnp.mean(x * x, axis=-1, keepdims=True)
    h = (x * lax.rsqrt(ms + EPS) * g_ref[...]).astype(BF16)

    def proj(off, width):
        return jnp.dot(h, w_ref[:, off:off + width], preferred_element_type=F32)

    cr, sr, cp, sp = cr_ref[...], sr_ref[...], cp_ref[...], sp_ref[...]
    chunk = lambda p, c: p[:, c * LANES:(c + 1) * LANES]
    qk_scale = RET_DK ** -0.5
    p_rqk = proj(_OFF["rq"], 2 * RET_QK)
    for c in range(RET_QK // LANES):
        sl = slice(c * LANES, (c + 1) * LANES)
        rq_ref[:, sl] = _rope_chunk(chunk(p_rqk, c), cr, sr, RET_DK // 2) * qk_scale
        rk_ref[:, sl] = _rope_chunk(chunk(p_rqk, RET_QK // LANES + c), cr, sr, RET_DK // 2)
    rv_ref[...] = proj(_OFF["rv"], RET_V).astype(BF16)
    rg_ref[...] = proj(_OFF["rg"], RET_V)
    tm = x.shape[0]
    dh_scale = DSA_DH ** -0.5
    p_dq, p_dk, p_dv = proj(_OFF["dq"], DSA_W), proj(_OFF["dk"], DSA_W), proj(_OFF["dv"], DSA_W)
    for c in range(DSA_W // LANES):
        sl = slice(c * LANES, (c + 1) * LANES)
        dq = _rope_chunk(chunk(p_dq, c), cp, sp, ROPE_DIM // 2) * dh_scale
        dqT_ref[sl, :] = dq.T.astype(BF16)
        dk_ref[:, sl] = _rope_chunk(chunk(p_dk, c), cp, sp, ROPE_DIM // 2).astype(BF16)
        dv = chunk(p_dv, c)
        for j in range(tm // KB):
            dvT_ref[j, sl, :] = dv[j * KB:(j + 1) * KB, :].T.astype(BF16)
    p_iq = proj(_OFF["iq"], IDX_Q)
    for c in range(IDX_Q // LANES):
        iq = _rope_chunk(chunk(p_iq, c), cp, sp, ROPE_DIM // 2)
        for j in range(tm // QB):
            t = iq[j * QB:(j + 1) * QB, :].T.astype(BF16)
            iqT_ref[j, :, (2 * c) * QB:(2 * c + 1) * QB] = t[:IDX_DH, :]
            iqT_ref[j, :, (2 * c + 1) * QB:(2 * c + 2) * QB] = t[IDX_DH:, :]
    p_ikw = proj(_OFF["ik"], 2 * LANES)
    ik = _rope_chunk(chunk(p_ikw, 0), cp, sp, ROPE_DIM // 2)
    ik_ref[...] = ik[:, :IDX_DH].astype(BF16)
    iw = chunk(p_ikw, 1) * ((IDX_HEADS ** -0.5) * (IDX_DH ** -0.5))
    iwT_ref[...] = iw.T[:SUBLANES, :]
    gate_ref[...] = proj(_OFF["gates"], 2 * d_model)


def _rope_tables(L, half, theta, pass_dims):
    inv = theta ** (-jnp.arange(half, dtype=F32) / half)
    ang = jnp.arange(L).astype(F32)[:, None] * inv[None, :]
    cos, sin = jnp.cos(ang), jnp.sin(ang)
    ones = jnp.ones((L, pass_dims), F32)
    zeros = jnp.zeros((L, pass_dims), F32)
    cos64 = jnp.concatenate([cos, cos, ones], axis=-1)
    sin64 = jnp.concatenate([-sin, sin, zeros], axis=-1)
    return jnp.tile(cos64, (1, 2)), jnp.tile(sin64, (1, 2))


def _inproj(x, attn_norm, w_in):
    B, L, D = x.shape
    tm = TM_IN
    nl = L // tm
    sizes = (RET_QK, RET_QK, RET_V, RET_V, DSA_W, DSA_W, DSA_W, IDX_Q, IDX_DH, IDX_HEADS, 2 * D)
    offs = np.concatenate([[0], np.cumsum(sizes)])
    cols = [w_in[:, offs[i]:offs[i + 1]] for i in range(len(sizes))]
    cols[8] = jnp.pad(cols[8], ((0, 0), (0, LANES - IDX_DH)))
    cols[9] = jnp.pad(cols[9], ((0, 0), (0, LANES - IDX_HEADS)))
    w = jnp.concatenate(cols, axis=1).astype(BF16)
    wtot = w.shape[1]
    cr, sr = _rope_tables(L, RET_DK // 2, RET_THETA, 0)
    cp, sp = _rope_tables(L, ROPE_DIM // 2, ROPE_THETA, DSA_DH - ROPE_DIM)

    tok = lambda width: pl.BlockSpec((None, tm, width), lambda b, l: (b, l, 0))
    tab = pl.BlockSpec((tm, LANES), lambda b, l: (l, 0))
    out_shape = (
        jax.ShapeDtypeStruct((B, L, RET_QK), F32),
        jax.ShapeDtypeStruct((B, L, RET_QK), F32),
        jax.ShapeDtypeStruct((B, L, RET_V), BF16),
        jax.ShapeDtypeStruct((B, L, RET_V), F32),
        jax.ShapeDtypeStruct((B, DSA_W, L), BF16),
        jax.ShapeDtypeStruct((B, L, DSA_W), BF16),
        jax.ShapeDtypeStruct((B, L // KB, DSA_W, KB), BF16),
        jax.ShapeDtypeStruct((B, L // QB, IDX_DH, IDX_HEADS * QB), BF16),
        jax.ShapeDtypeStruct((B, L, IDX_DH), BF16),
        jax.ShapeDtypeStruct((B, SUBLANES, L), F32),
        jax.ShapeDtypeStruct((B, L, 2 * D), F32),
    )
    out_specs = (
        tok(RET_QK), tok(RET_QK), tok(RET_V), tok(RET_V),
        pl.BlockSpec((None, DSA_W, tm), lambda b, l: (b, 0, l)),
        tok(DSA_W),
        pl.BlockSpec((None, tm // KB, DSA_W, KB), lambda b, l: (b, l, 0, 0)),
        pl.BlockSpec((None, tm // QB, IDX_DH, IDX_HEADS * QB), lambda b, l: (b, l, 0, 0)),
        tok(IDX_DH),
        pl.BlockSpec((None, SUBLANES, tm), lambda b, l: (b, 0, l)),
        tok(2 * D),
    )
    return pl.pallas_call(
        functools.partial(_inproj_kernel, d_model=D),
        grid=(B, nl),
        in_specs=[tok(D), pl.BlockSpec((1, D), lambda b, l: (0, 0)),
                  pl.BlockSpec((D, wtot), lambda b, l: (0, 0)), tab, tab, tab, tab],
        out_specs=out_specs,
        out_shape=out_shape,
        compiler_params=_params(("parallel", "parallel")),
        name="inproj",
    )(x, attn_norm.reshape(1, D), w, cr, sr, cp, sp)


def _retention_kernel(q_ref, k_ref, v_ref, g_ref, gn_ref, dm_ref, qd_ref, kd_ref, cd_ref, y_ref, state_ref):
    @pl.when(pl.program_id(1) == 0)
    def _():
        state_ref[...] = jnp.zeros_like(state_ref)

    for h in range(RET_HEADS):
        q = q_ref[:, h * RET_DK:(h + 1) * RET_DK]
        k = k_ref[:, h * RET_DK:(h + 1) * RET_DK]
        v = v_ref[:, h * RET_DV:(h + 1) * RET_DV]
        s = lax.dot_general(q.astype(BF16), k.astype(BF16), (((1,), (1,)), ((), ())), preferred_element_type=F32)
        intra = jnp.dot((s * dm_ref[h]).astype(BF16), v, preferred_element_type=F32)
        state = state_ref[h]
        cross = jnp.dot((q * qd_ref[h]).astype(BF16), state.astype(BF16), preferred_element_type=F32)
        kv = lax.dot_general((k * kd_ref[h]).astype(BF16), v, (((0,), (0,)), ((), ())), preferred_element_type=F32)
        state_ref[h] = cd_ref[h] * state + kv
        y = intra + cross
        mu = jnp.mean(y, axis=-1, keepdims=True)
        yc = y - mu
        var = jnp.mean(yc * yc, axis=-1, keepdims=True)
        sl = slice(h * RET_DV, (h + 1) * RET_DV)
        yn = yc * lax.rsqrt(var + EPS) * gn_ref[:, sl]
        g = g_ref[:, sl]
        y_ref[:, sl] = (g * jax.nn.sigmoid(g) * yn).astype(y_ref.dtype)


def _retention(rq, rk, rv, rg, ret_gn):
    B, L, _ = rq.shape
    tb = TB_RET
    log_g = jnp.log(1.0 - 2.0 ** (-5.0 - jnp.arange(RET_HEADS, dtype=F32)))
    i = jnp.arange(tb, dtype=F32)
    ci = jnp.arange(tb) // CHUNK
    visible = (ci[None, :] <= ci[:, None]).astype(F32)
    dm = jnp.exp(log_g[:, None, None] * jnp.abs(i[:, None] - i[None, :])) * visible[None]
    ones = jnp.ones((1, 1, RET_DK), F32)
    qd = jnp.exp(log_g[:, None] * (i + 1.0)[None, :])[:, :, None] * ones
    kd = jnp.exp(log_g[:, None] * (tb - 1.0 - i)[None, :])[:, :, None] * ones
    cd = jnp.exp(log_g * tb)[:, None, None] * jnp.ones((1, RET_DK, RET_DV), F32)

    tok = lambda width: pl.BlockSpec((None, tb, width), lambda b, l: (b, l, 0))
    full = lambda shape: pl.BlockSpec(shape, lambda b, l: (0,) * len(shape))
    return pl.pallas_call(
        _retention_kernel,
        grid=(B, L // tb),
        in_specs=[tok(RET_QK), tok(RET_QK), tok(RET_V), tok(RET_V), full((1, RET_V)),
                  full((RET_HEADS, tb, tb)), full((RET_HEADS, tb, RET_DK)), full((RET_HEADS, tb, RET_DK)),
                  full((RET_HEADS, RET_DK, RET_DV))],
        out_specs=tok(RET_V),
        out_shape=jax.ShapeDtypeStruct((B, L, RET_V), BF16),
        scratch_shapes=[pltpu.VMEM((RET_HEADS, RET_DK, RET_DV), F32)],
        compiler_params=_params(("parallel", "arbitrary")),
        name="retention",
    )(rq, rk, rv, rg, ret_gn.reshape(1, RET_V), dm, qd, kd, cd)


def _dsa_kernel(iqT_ref, iwT_ref, ik_ref, dqT_ref, dk_ref, dvT_ref, tri_ref, y_ref,
                keys_ref, rhs_ref, m_ref, l_ref, acc_ref, s_ref, p_ref, alpha_ref, smax_ref, *, topk):
    qb = pl.program_id(1)
    nkb = (qb * QB) // KB + 1
    n_trips = (nkb + 1) // 2
    n_pairs = DSA_HEADS // 2

    key_row = lax.broadcasted_iota(I32, (KB, QB), 0)
    q_chunk = (qb * QB + lax.broadcasted_iota(I32, (KB, QB), 1)) // CHUNK

    def score_block(kb):
        k0 = pl.multiple_of(kb * KB, KB)
        r = jnp.dot(ik_ref[pl.ds(k0, KB), :], iqT_ref[...], preferred_element_type=F32)
        acc = jnp.zeros((KB, QB), F32)
        for h in range(IDX_HEADS):
            acc = acc + iwT_ref[h:h + 1, :] * jnp.maximum(r[:, h * QB:(h + 1) * QB], 0.0)
        acc = acc + 0.0
        bits = pltpu.bitcast(acc, I32)
        key = jnp.where(bits < 0, bits ^ 0x7FFFFFFF, bits)
        admissible = (k0 + key_row) // CHUNK <= q_chunk
        keys_ref[kb] = jnp.where(admissible, key, INT_MIN)

    def score_two_blocks(t, carry):
        score_block(2 * t)
        score_block(2 * t + 1)
        return carry

    lax.fori_loop(0, n_trips, score_two_blocks, 0)

    def count_ge(cand):
        def body(t, cnt):
            for kb in (2 * t, 2 * t + 1):
                hit = jnp.where(keys_ref[kb] >= cand, 1.0, 0.0)
                cnt = cnt + jnp.sum(hit.reshape(KB // SM_ROWS, SM_ROWS, QB), axis=0)
            return cnt

        cnt = lax.fori_loop(0, n_trips, body, jnp.zeros((SM_ROWS, QB), F32))
        return jnp.sum(cnt, axis=0, keepdims=True).astype(I32)

    zero = jnp.zeros((1, QB), I32)
    t0 = jnp.where(count_ge(zero) >= topk, zero, zero + INT_MIN)

    def bit_step(n, t):
        cand = t + jnp.left_shift(jnp.int32(1), 30 - n)
        return jnp.where(count_ge(cand) >= topk, cand, t)

    thr = lax.fori_loop(0, 31, bit_step, t0)
    n_gt = count_ge(thr + 1)
    need = jnp.where(thr == INT_MIN, 0, topk - n_gt).astype(F32)

    d_row = lax.broadcasted_iota(I32, (LANES, QB), 0)
    eye = jnp.where(d_row == lax.broadcasted_iota(I32, (LANES, QB), 1), 1.0, 0.0).astype(BF16)
    for j in range(n_pairs):
        qT = dqT_ref[j * LANES:(j + 1) * LANES, :]
        rhs_ref[j, :LANES, :QB] = jnp.where(d_row < DSA_DH, qT, jnp.zeros_like(qT))
        rhs_ref[j, :LANES, QB:] = jnp.where(d_row >= DSA_DH, qT, jnp.zeros_like(qT))
        rhs_ref[j, LANES:, :QB] = eye
        rhs_ref[j, LANES:, QB:] = eye
    m_ref[...] = jnp.full(m_ref.shape, M_INIT, F32)
    l_ref[...] = jnp.zeros(l_ref.shape, F32)
    acc_ref[...] = jnp.zeros(acc_ref.shape, F32)

    def logits(blk, slot, n_eq_before):
        kb = jnp.minimum(blk, nkb - 1)
        k0 = pl.multiple_of(kb * KB, KB)
        key = jnp.where(blk < nkb, keys_ref[kb], INT_MIN)
        eq = key == thr
        eq_f = jnp.where(eq, 1.0, 0.0)
        prefix = jnp.dot(tri_ref[...], eq_f.astype(BF16), preferred_element_type=F32)
        keep_tie = (n_eq_before + prefix) < need
        bias = jnp.where(key > thr, 0.0, jnp.where(eq, jnp.where(keep_tie, 0.0, NEG_BIG), NEG_BIG)).astype(BF16)
        for j in range(n_pairs):
            lhs = jnp.concatenate([dk_ref[pl.ds(k0, KB), j * LANES:(j + 1) * LANES], bias], axis=1)
            s = jnp.dot(lhs, rhs_ref[j], preferred_element_type=F32)
            s_ref[slot, j] = s
            smax_ref[slot, j] = jnp.max(s, axis=0, keepdims=True)
        return n_eq_before + jnp.sum(eq_f, axis=0, keepdims=True)

    def softmax_update(slot):
        rows = [slice(r * SM_ROWS, (r + 1) * SM_ROWS) for r in range(KB // SM_ROWS)]
        for j in range(n_pairs):
            m_prev = m_ref[j]
            m_new = jnp.maximum(m_prev, smax_ref[slot, j])
            alpha = jnp.exp(m_prev - m_new)
            psum = jnp.zeros((SM_ROWS, 2 * QB), F32)
            for r in rows:
                p = jnp.exp(s_ref[slot, j, r, :] - m_new)
                psum = psum + p
                p_ref[slot, j, r, :] = p.astype(BF16)
            l_ref[j] = alpha * l_ref[j] + jnp.sum(psum, axis=0, keepdims=True)
            m_ref[j] = m_new
            alpha_ref[slot, j] = alpha

    def accumulate(kb, slot):
        for j in range(n_pairs):
            pv = jnp.dot(dvT_ref[kb, j * LANES:(j + 1) * LANES, :], p_ref[slot, j], preferred_element_type=F32)
            acc_ref[j] = alpha_ref[slot, j] * acc_ref[j] + pv

    p_ref[1] = jnp.zeros(p_ref.shape[1:], BF16)
    alpha_ref[1] = jnp.ones(alpha_ref.shape[1:], F32)
    n_eq0 = logits(0, 0, jnp.zeros((1, QB), F32))

    def attend_two_blocks(t, n_eq):
        b = 2 * t
        n_eq = logits(b + 1, 1, n_eq)
        softmax_update(0)
        accumulate(jnp.maximum(b - 1, 0), 1)
        n_eq = logits(b + 2, 0, n_eq)
        softmax_update(1)
        accumulate(b, 0)
        return n_eq

    lax.fori_loop(0, n_trips, attend_two_blocks, n_eq0)
    accumulate(jnp.minimum(2 * n_trips - 1, nkb - 1), 1)

    outs = []
    for j in range(n_pairs):
        acc, l = acc_ref[j], l_ref[j]
        outs.append(acc[:DSA_DH, :QB] / l[:, :QB])
        outs.append(acc[DSA_DH:, QB:] / l[:, QB:])
    y_ref[...] = jnp.concatenate(outs, axis=0).T.astype(y_ref.dtype)


def _dsa(iqT, iwT, ik, dqT, dk, dvT):
    B, L, _ = dk.shape
    topk = min(DSA_TOPK_MAX, L // 4)
    r = jnp.arange(KB)
    tri = (r[None, :] < r[:, None]).astype(BF16)
    per_batch = lambda shape: pl.BlockSpec((None,) + shape, lambda b, q: (b,) + (0,) * len(shape))
    return pl.pallas_call(
        functools.partial(_dsa_kernel, topk=topk),
        grid=(B, L // QB),
        in_specs=[pl.BlockSpec((None, None, IDX_DH, IDX_HEADS * QB), lambda b, q: (b, q, 0, 0)),
                  pl.BlockSpec((None, SUBLANES, QB), lambda b, q: (b, 0, q)),
                  per_batch((L, IDX_DH)),
                  pl.BlockSpec((None, DSA_W, QB), lambda b, q: (b, 0, q)),
                  per_batch((L, DSA_W)),
                  per_batch((L // KB, DSA_W, KB)),
                  pl.BlockSpec((KB, KB), lambda b, q: (0, 0))],
        out_specs=pl.BlockSpec((None, QB, DSA_W), lambda b, q: (b, q, 0)),
        out_shape=jax.ShapeDtypeStruct((B, L, DSA_W), BF16),
        scratch_shapes=[pltpu.VMEM((L // KB, KB, QB), I32),
                        pltpu.VMEM((DSA_HEADS // 2, 2 * LANES, 2 * QB), BF16),
                        pltpu.VMEM((DSA_HEADS // 2, 1, 2 * QB), F32),
                        pltpu.VMEM((DSA_HEADS // 2, 1, 2 * QB), F32),
                        pltpu.VMEM((DSA_HEADS // 2, LANES, 2 * QB), F32),
                        pltpu.VMEM((2, DSA_HEADS // 2, KB, 2 * QB), F32),
                        pltpu.VMEM((2, DSA_HEADS // 2, KB, 2 * QB), BF16),
                        pltpu.VMEM((2, DSA_HEADS // 2, 1, 2 * QB), F32),
                        pltpu.VMEM((2, DSA_HEADS // 2, 1, 2 * QB), F32)],
        compiler_params=_params(("parallel", "arbitrary")),
        name="dsa",
    )(iqT, iwT, ik, dqT, dk, dvT, tri)


def _merge_kernel(yr_ref, yd_ref, gate_ref, x_ref, wr_ref, wd_ref, wo_ref, fn_ref, x1_ref, h2T_ref, *, d_model):
    g = gate_ref[...]
    br = jnp.dot(yr_ref[...], wr_ref[...], preferred_element_type=F32)
    bd = jnp.dot(yd_ref[...], wd_ref[...], preferred_element_type=F32)
    merged = jax.nn.sigmoid(g[:, :d_model]) * br + jax.nn.sigmoid(g[:, d_model:]) * bd
    x1 = x_ref[...] + jnp.dot(merged.astype(BF16), wo_ref[...], preferred_element_type=F32)
    x1_ref[...] = x1
    ms = jnp.mean(x1 * x1, axis=-1, keepdims=True)
    h2 = x1 * lax.rsqrt(ms + EPS) * fn_ref[...]
    h2T_ref[...] = h2.T.astype(BF16)


def _merge(y_ret, y_dsa, gates, x, w_ret_o, w_dsa_o, w_out, ffn_norm):
    N, D = x.shape
    tm = TM_MERGE
    tok = lambda width: pl.BlockSpec((tm, width), lambda t: (t, 0))
    full = lambda shape: pl.BlockSpec(shape, lambda t: (0, 0))
    return pl.pallas_call(
        functools.partial(_merge_kernel, d_model=D),
        grid=(N // tm,),
        in_specs=[tok(RET_V), tok(DSA_W), tok(2 * D), tok(D),
                  full((RET_V, D)), full((DSA_W, D)), full((D, D)), full((1, D))],
        out_specs=(tok(D), pl.BlockSpec((D, tm), lambda t: (0, t))),
        out_shape=(jax.ShapeDtypeStruct((N, D), F32), jax.ShapeDtypeStruct((D, N), BF16)),
        compiler_params=_params(("parallel",)),
        name="merge",
    )(y_ret, y_dsa, gates, x, w_ret_o.astype(BF16), w_dsa_o.astype(BF16), w_out.astype(BF16),
      ffn_norm.reshape(1, D))


_ROW_LEN = [PEER_TOPK // (a + 1) for a in range(PEER_TOPK)]


def _extract_rounds(s, n, break_ties):
    rows = lax.broadcasted_iota(I32, s.shape, 0).astype(F32)
    rank = jnp.full(s.shape, float(n), F32)
    vals = []
    for a in range(n):
        m = jnp.max(s, axis=0, keepdims=True)
        hit = s == m
        if break_ties:
            first = jnp.min(jnp.where(hit, rows, float(s.shape[0])), axis=0, keepdims=True)
            hit = rows == first
        s = jnp.where(hit, -jnp.inf, s)
        rank = jnp.where(hit, float(a), rank)
        vals.append(m)
    return vals, rank


def _extract_top_pair(s_a, s_b, n):
    fast = (_extract_rounds(s_a, n, break_ties=False), _extract_rounds(s_b, n, break_ties=False))
    taken = sum(jnp.sum(jnp.where(rank < float(n), 1.0, 0.0), axis=0, keepdims=True) for _, rank in fast)
    return lax.cond(jnp.max(taken) > float(2 * n),
                    lambda: (_extract_rounds(s_a, n, break_ties=True), _extract_rounds(s_b, n, break_ties=True)),
                    lambda: fast)


def _peer_route_kernel(h2T_ref, wqT_ref, sk_ref, r1_ref, p1_ref, n_ref, p0_ref):
    h2T = h2T_ref[...]
    K = PEER_TOPK
    for h in range(PEER_HEADS):
        sT = []
        for c in range(2):
            r0 = (h * 2 + c) * PEER_NKEYS
            qT = jnp.dot(wqT_ref[r0:r0 + PEER_NKEYS, :], h2T, preferred_element_type=F32)
            sT.append(jnp.dot(sk_ref[h, c], qT.astype(BF16), preferred_element_type=F32))
        (v0, rank0), (v1, rank1) = _extract_top_pair(sT[0], sT[1], K)
        cand = [[v0[a] + v1[b] for b in range(_ROW_LEN[a])] for a in range(K)]
        best, _ = _extract_rounds(jnp.concatenate([c for row in cand for c in row], axis=0), K, break_ties=True)
        tau = best[K - 1]
        z = jnp.zeros_like(tau)
        for k in range(K):
            z = z + jnp.exp(best[k] - best[0])
        gt = [sum(jnp.where(c > tau, 1.0, 0.0) for c in row) for row in cand]
        eq = [sum(jnp.where(c == tau, 1.0, 0.0) for c in row) for row in cand]
        need = float(K) - sum(gt)
        n = jnp.zeros(rank0.shape, F32)
        for a in range(K):
            n_a = gt[a] + jnp.minimum(jnp.maximum(need, 0.0), eq[a])
            need = need - eq[a]
            n = jnp.where(rank0 == float(a), n_a, n)
        r1_ref[h] = rank1.astype(BF16)
        p1_ref[h] = jnp.exp(sT[1] - v1[0]).astype(BF16)
        n_ref[h] = n.astype(BF16)
        p0_ref[h] = (0.5 * jnp.exp(sT[0] - v0[0]) / z).astype(BF16)


def _peer_route(h2T, peer_wq, peer_subkeys):
    D, N = h2T.shape
    tm = TM_ROUTE
    wqT = peer_wq.T.astype(BF16)
    sk = peer_subkeys.astype(BF16)
    spec = pl.BlockSpec((PEER_HEADS, PEER_NKEYS, tm), lambda t: (0, 0, t))
    shape = lambda dt: jax.ShapeDtypeStruct((PEER_HEADS, PEER_NKEYS, N), dt)
    return pl.pallas_call(
        _peer_route_kernel,
        grid=(N // tm,),
        in_specs=[pl.BlockSpec((D, tm), lambda t: (0, t)),
                  pl.BlockSpec(wqT.shape, lambda t: (0, 0)),
                  pl.BlockSpec(sk.shape, lambda t: (0, 0, 0, 0))],
        out_specs=(spec, spec, spec, spec),
        out_shape=(shape(BF16), shape(BF16), shape(BF16), shape(BF16)),
        compiler_params=_params(("parallel",)),
        name="peer_route",
    )(h2T, wqT, sk)


def _peer_dense_kernel(h2T_ref, r1_ref, p1_ref, n_ref, p0_ref, u_ref, vT_ref, x1_ref, fn_ref, out_ref,
                       acc_ref, a_ref, g_ref, r1s_ref, p1s_ref, *, n_groups):
    step = pl.program_id(1)
    tm = h2T_ref.shape[1]

    def project(slot):
        piece = u_ref.shape[0] // MXU_PIECES
        for p in range(MXU_PIECES):
            rows = slice(p * piece, (p + 1) * piece)
            a_ref[slot, rows, :] = jnp.dot(u_ref[rows, :], h2T_ref[...], preferred_element_type=F32)

    def gate(slot):
        for c in range(tm // LANES):
            sl = slice(c * LANES, (c + 1) * LANES)
            for il in range(IG_PEER):
                rows = slice(il * PEER_NKEYS, (il + 1) * PEER_NKEYS)
                w = jnp.zeros((PEER_NKEYS, LANES), BF16)
                for h in range(PEER_HEADS):
                    n_i = n_ref[h, il:il + 1, sl]
                    p0 = p0_ref[h, il:il + 1, sl]
                    w = w + jnp.where(r1s_ref[h, :, sl] < n_i, p1s_ref[h, :, sl] * p0, jnp.zeros_like(w))
                a = a_ref[slot, rows, sl]
                gelu2 = a * (1.0 + lax.erf(a * (2.0 ** -0.5)))
                g_ref[slot, rows, sl] = (w.astype(F32) * gelu2).astype(BF16)

    def combine(slot):
        acc_ref[...] += jnp.dot(vT_ref[...], g_ref[slot], preferred_element_type=F32)

    @pl.when(step == 0)
    def _():
        acc_ref[...] = jnp.zeros_like(acc_ref)
        g_ref[1] = jnp.zeros(g_ref.shape[1:], BF16)
        r1s_ref[...] = r1_ref[...]
        p1s_ref[...] = p1_ref[...]
        project(0)

    @pl.when(jnp.logical_and(step > 0, step <= n_groups))
    def _():
        project(step % 2)
        gate((step - 1) % 2)
        combine(step % 2)

    @pl.when(step == n_groups + 1)
    def _():
        combine(step % 2)
        y = x1_ref[...] + acc_ref[...].T
        ms = jnp.mean(y * y, axis=-1, keepdims=True)
        out_ref[...] = y * lax.rsqrt(ms + EPS) * fn_ref[...]


def _peer_dense(h2T, r1, p1, n, p0, peer_u, peer_v, x1, final_norm):
    D, N = h2T.shape
    tm = TM_PEER
    n_exp = peer_u.shape[0]
    u = peer_u.astype(BF16)
    vT = peer_v.T.astype(BF16)
    rows = IG_PEER * PEER_NKEYS
    ng = n_exp // rows
    rt = pl.BlockSpec((PEER_HEADS, PEER_NKEYS, tm), lambda t, s: (0, 0, t))
    grp = pl.BlockSpec((PEER_HEADS, None, IG_PEER, tm), lambda t, s: (0, jnp.clip(s - 1, 0, ng - 1), 0, t))
    n = n.reshape(PEER_HEADS, ng, IG_PEER, N)
    p0 = p0.reshape(PEER_HEADS, ng, IG_PEER, N)
    return pl.pallas_call(
        functools.partial(_peer_dense_kernel, n_groups=ng),
        grid=(N // tm, ng + 2),
        in_specs=[pl.BlockSpec((D, tm), lambda t, s: (0, t)), rt, rt, grp, grp,
                  pl.BlockSpec((rows, D), lambda t, s: (jnp.minimum(s, ng - 1), 0)),
                  pl.BlockSpec((D, rows), lambda t, s: (0, jnp.clip(s - 2, 0, ng - 1))),
                  pl.BlockSpec((tm, D), lambda t, s: (t, 0)),
                  pl.BlockSpec((1, D), lambda t, s: (0, 0))],
        out_specs=pl.BlockSpec((tm, D), lambda t, s: (t, 0)),
        out_shape=jax.ShapeDtypeStruct((N, D), F32),
        scratch_shapes=[pltpu.VMEM((D, tm), F32), pltpu.VMEM((2, rows, tm), F32), pltpu.VMEM((2, rows, tm), BF16),
                        pltpu.VMEM((PEER_HEADS, PEER_NKEYS, tm), BF16), pltpu.VMEM((PEER_HEADS, PEER_NKEYS, tm), BF16)],
        compiler_params=_params(("parallel", "arbitrary")),
        name="peer_dense",
    )(h2T, r1, p1, n, p0, u, vT, x1, final_norm.reshape(1, D))


def kernel(x, attn_norm, w_in, ret_gn, w_ret_o, w_dsa_o, w_out, ffn_norm, peer_wq, peer_subkeys, peer_u, peer_v,
           final_norm):
    B, L, D = x.shape
    assert attn_norm.shape[0] == 1, "single-layer block"
    assert L % TM_IN == 0 and (B * L) % TM_PEER == 0 and D % LANES == 0 and QB == LANES
    rq, rk, rv, rg, dqT, dk, dvT, iqT, ik, iwT, gates = _inproj(x, attn_norm[0], w_in[0])
    y_ret = _retention(rq, rk, rv, rg, ret_gn[0])
    y_dsa = _dsa(iqT, iwT, ik, dqT, dk, dvT)
    x1, h2T = _merge(y_ret.reshape(B * L, RET_V), y_dsa.reshape(B * L, DSA_W), gates.reshape(B * L, 2 * D),
                     x.reshape(B * L, D), w_ret_o[0], w_dsa_o[0], w_out[0], ffn_norm[0])
    r1, p1, n, p0 = _peer_route(h2T, peer_wq[0], peer_subkeys[0])
    out = _peer_dense(h2T, r1, p1, n, p0, peer_u[0], peer_v[0], x1, final_norm)
    return out.reshape(B, L, D)
```
